```python
import math
import jax, jax.numpy as jnp
from jax import lax
import numpy as np

D_MODEL = 1024
BATCH = 32
SEQ = 2048
DEPTH = 2
DEC_BATCH = 8
DEC_SEQ = 8192
PAST_LEN = 128

EPS = 1e-6
CONV_W = 5
CHUNK = 64
Q_BLOCK = 128

GDN_HEADS = 4
GDN_DK = 128
GDN_DV = 128
GDN_QK_W = GDN_HEADS * GDN_DK
GDN_V_W = GDN_HEADS * GDN_DV
GDN_QKV_W = 2 * GDN_QK_W + GDN_V_W
GDN_COLS = GDN_QKV_W + GDN_V_W + 4 * GDN_HEADS

SSD_HEADS = 8
SSD_HEAD_DIM = 64
SSD_INNER = SSD_HEADS * SSD_HEAD_DIM
SSD_GROUPS = 2
SSD_STATE = 128
SSD_XBC = SSD_INNER + 2 * SSD_GROUPS * SSD_STATE
SSD_COLS = SSD_INNER + SSD_XBC + 2 * SSD_HEADS

MLA_HEADS = 4
MLA_Q_LORA = 256
MLA_KV_LORA = 256
MLA_NOPE = 128
MLA_ROPE = 64
MLA_V = 128
MLA_COLS = MLA_Q_LORA + MLA_KV_LORA + MLA_ROPE
MLA_SCALE = (MLA_NOPE + MLA_ROPE) ** -0.5
ROPE_BASE = 10000.0

IN_COLS = GDN_COLS + SSD_COLS + MLA_COLS
N_BRANCH = 3
FFN_HIDDEN = 2816
DT_MIN = 0.001
DT_MAX = 0.1

kernel_name = "hybrid_gdn_ssd_mla_macaron_encoder"

F32 = jnp.float32


def rmsnorm(x, g):
    xf = x.astype(F32)
    y = xf * lax.rsqrt(jnp.mean(xf * xf, axis=-1, keepdims=True) + EPS)
    return (y * g.astype(F32)).astype(x.dtype)


def l2norm(x):
    xf = x.astype(F32)
    return xf * lax.rsqrt(jnp.sum(xf * xf, axis=-1, keepdims=True) + EPS)


def swiglu(x, w_in, w_out):
    g, u = jnp.split(x @ w_in, 2, axis=-1)
    return (jax.nn.silu(g) * u) @ w_out


def depthwise_conv(x, w):
    return lax.conv_general_dilated(
        x, w.astype(x.dtype)[:, None, :], window_strides=(1,),
        padding=((CONV_W // 2, CONV_W // 2),),
        dimension_numbers=("NWC", "WIO", "NWC"), feature_group_count=x.shape[-1])


def rope_tables(L, dtype):
    inv_freq = jnp.power(ROPE_BASE, -jnp.arange(0, MLA_ROPE, 2, dtype=F32) / MLA_ROPE)
    ang = jnp.arange(L, dtype=F32)[:, None] * inv_freq[None, :]
    ang = jnp.concatenate([ang, ang], axis=-1)
    return jnp.cos(ang).astype(dtype), jnp.sin(ang).astype(dtype)


def rope(x, cos, sin):
    x1, x2 = jnp.split(x, 2, axis=-1)
    return x * cos + jnp.concatenate([-x2, x1], axis=-1) * sin


def gated_delta_chunked(q, k, v, g, beta):
    Bsz, L, H, DK = q.shape
    DV = v.shape[-1]
    N = L // CHUNK

    def to_chunks(t):
        t = t.reshape(Bsz, N, CHUNK, H, *t.shape[3:])
        return jnp.moveaxis(t, 3, 1)

    q = to_chunks(q * DK ** -0.5)
    k = to_chunks(k)
    v = to_chunks(v)
    beta = to_chunks(beta)
    gc = jnp.cumsum(to_chunks(g), axis=-1)
    idx = jnp.arange(CHUNK)
    lower = idx[:, None] >= idx[None, :]
    strict = idx[:, None] > idx[None, :]
    decay = jnp.exp(jnp.where(lower, gc[..., :, None] - gc[..., None, :], -jnp.inf))
    kb = k * beta[..., None]
    a_mat = jnp.where(strict, jnp.einsum("bhnid,bhnjd->bhnij", kb, k) * decay, 0.0)
    eye = jnp.eye(CHUNK, dtype=F32)
    rhs = jnp.concatenate([v * beta[..., None], kb * jnp.exp(gc)[..., None]], axis=-1)
    sol = lax.linalg.triangular_solve(eye + a_mat, rhs, left_side=True, lower=True,
                                      unit_diagonal=True)
    u, w = sol[..., :DV], sol[..., DV:]
    attn = jnp.einsum("bhnid,bhnjd->bhnij", q, k) * decay
    q_dec = q * jnp.exp(gc)[..., None]
    g_last = gc[..., -1]
    k_dec = k * jnp.exp(g_last[..., None] - gc)[..., None]

    def step(S, xs):
        u_c, w_c, q_c, k_c, a_c, gl = xs
        v_new = u_c - w_c @ S
        o = q_c @ S + a_c @ v_new
        S = S * jnp.exp(gl)[..., None, None] + jnp.swapaxes(k_c, -1, -2) @ v_new
        return S, o

    xs = tuple(jnp.moveaxis(t, 2, 0) for t in (u, w, q_dec, k_dec, attn, g_last))
    S0 = jnp.zeros((Bsz, H, DK, DV), F32)
    _, o = lax.scan(step, S0, xs)
    return o.transpose(1, 0, 3, 2, 4).reshape(Bsz, L, H, DV)


def gdn_branch(cols, conv_w, A_log, dt_bias, norm_g):
    Bsz, L, _ = cols.shape
    qkv = jax.nn.silu(depthwise_conv(cols[..., :GDN_QKV_W], conv_w))
    z = cols[..., GDN_QKV_W:GDN_QKV_W + GDN_V_W].reshape(Bsz, L, GDN_HEADS, GDN_DV)
    ab = cols[..., GDN_QKV_W + GDN_V_W:].astype(F32).reshape(Bsz, L, 4, GDN_HEADS)
    q = l2norm(qkv[..., :GDN_QK_W].reshape(Bsz, L, GDN_HEADS, GDN_DK))
    k = l2norm(qkv[..., GDN_QK_W:2 * GDN_QK_W].reshape(Bsz, L, GDN_HEADS, GDN_DK))
    v = qkv[..., 2 * GDN_QK_W:].astype(F32).reshape(Bsz, L, GDN_HEADS, GDN_DV)
    g = -jnp.exp(A_log.astype(F32)) * jax.nn.softplus(ab[:, :, 0:2] + dt_bias.astype(F32))
    beta = jax.nn.sigmoid(ab[:, :, 2:4])
    flip = lambda t: jnp.flip(t, axis=1)
    o_f = gated_delta_chunked(q, k, v, g[:, :, 0], beta[:, :, 0])
    o_b = flip(gated_delta_chunked(flip(q), flip(k), flip(v), flip(g[:, :, 1]), flip(beta[:, :, 1])))
    o = rmsnorm(o_f + o_b, norm_g) * jax.nn.silu(z.astype(F32))
    return o.reshape(Bsz, L, GDN_V_W).astype(cols.dtype)


def ssd_chunked(x, dt, A, Bm, Cm):
    Bsz, L, H, P = x.shape
    G, S = Bm.shape[2], Bm.shape[3]
    R = H // G
    N = L // CHUNK
    xr = (x * dt[..., None]).reshape(Bsz, N, CHUNK, G, R, P)
    ac = jnp.cumsum((dt * A).reshape(Bsz, N, CHUNK, G, R), axis=2)
    Bc = Bm.reshape(Bsz, N, CHUNK, G, S)
    Cc = Cm.reshape(Bsz, N, CHUNK, G, S)
    idx = jnp.arange(CHUNK)
    lower = (idx[:, None] >= idx[None, :])[:, :, None, None]
    seg = jnp.exp(jnp.where(lower, ac[:, :, :, None] - ac[:, :, None, :], -jnp.inf))
    cb = jnp.einsum("bnigs,bnjgs->bnijg", Cc, Bc)
    y_diag = jnp.einsum("bnijg,bnijgr,bnjgrp->bnigrp", cb, seg, xr)
    decay_states = jnp.exp(ac[:, :, -1:] - ac)
    states = jnp.einsum("bncgs,bncgr,bncgrp->bngrps", Bc, decay_states, xr)
    chunk_decay = jnp.exp(ac[:, :, -1])

    def step(h, inp):
        st, dec = inp
        return h * dec[..., None, None] + st, h

    h0 = jnp.zeros((Bsz, G, R, P, S), F32)
    _, h_in = lax.scan(step, h0, (jnp.moveaxis(states, 1, 0), jnp.moveaxis(chunk_decay, 1, 0)))
    h_in = jnp.moveaxis(h_in, 0, 1)
    y_off = jnp.einsum("bncgs,bngrps,bncgr->bncgrp", Cc, h_in, jnp.exp(ac))
    return (y_diag + y_off).reshape(Bsz, L, H, P)


def ssd_branch(cols, conv_w, conv_b, A_log, dt_bias, D_skip, norm_g):
    Bsz, L, _ = cols.shape
    z = cols[..., :SSD_INNER]
    xbc = jax.nn.silu(depthwise_conv(cols[..., SSD_INNER:SSD_INNER + SSD_XBC], conv_w) + conv_b.astype(cols.dtype))
    xbc = xbc.astype(F32)
    xh = xbc[..., :SSD_INNER].reshape(Bsz, L, SSD_HEADS, SSD_HEAD_DIM)
    Bm = xbc[..., SSD_INNER:SSD_INNER + SSD_GROUPS * SSD_STATE].reshape(Bsz, L, SSD_GROUPS, SSD_STATE)
    Cm = xbc[..., SSD_INNER + SSD_GROUPS * SSD_STATE:].reshape(Bsz, L, SSD_GROUPS, SSD_STATE)
    dt = cols[..., SSD_INNER + SSD_XBC:].astype(F32).reshape(Bsz, L, 2, SSD_HEADS)
    dt = jax.nn.softplus(dt + dt_bias.astype(F32))
    A = -jnp.exp(A_log.astype(F32))
    flip = lambda t: jnp.flip(t, axis=1)
    y_f = ssd_chunked(xh, dt[:, :, 0], A[0], Bm, Cm)
    y_b = flip(ssd_chunked(flip(xh), flip(dt[:, :, 1]), A[1], flip(Bm), flip(Cm)))
    y = y_f + y_b + xh * D_skip.astype(F32)[:, None]
    y = rmsnorm(y.reshape(Bsz, L, SSD_INNER) * jax.nn.silu(z.astype(F32)), norm_g)
    return y.astype(cols.dtype)


def mla_branch(cols, cos, sin, q_norm, w_uq, kv_norm, w_ukv):
    Bsz, L, _ = cols.shape
    cq = cols[..., :MLA_Q_LORA]
    ckv = cols[..., MLA_Q_LORA:MLA_Q_LORA + MLA_KV_LORA]
    k_rope = rope(cols[..., MLA_Q_LORA + MLA_KV_LORA:], cos, sin)
    q = (rmsnorm(cq, q_norm) @ w_uq).reshape(Bsz, L, MLA_HEADS, MLA_NOPE + MLA_ROPE)
    kv = (rmsnorm(ckv, kv_norm) @ w_ukv).reshape(Bsz, L, MLA_HEADS, MLA_NOPE + MLA_V)
    q_nope = q[..., :MLA_NOPE]
    q_rope = rope(q[..., MLA_NOPE:], cos[:, None], sin[:, None])
    k_nope = kv[..., :MLA_NOPE]
    v = kv[..., MLA_NOPE:]
    nb = L // Q_BLOCK

    def blocks(t):
        return jnp.moveaxis(t.reshape(Bsz, nb, Q_BLOCK, *t.shape[2:]), 1, 0)

    def attend(qb):
        qn, qr = qb
        s = jnp.einsum("bqhd,bkhd->bhqk", qn, k_nope) + jnp.einsum("bqhr,bkr->bhqk", qr, k_rope)
        p = jax.nn.softmax(s.astype(F32) * MLA_SCALE, axis=-1).astype(v.dtype)
        return jnp.einsum("bhqk,bkhd->bqhd", p, v)

    o = lax.map(attend, (blocks(q_nope), blocks(q_rope)))
    return jnp.moveaxis(o, 0, 1).reshape(Bsz, L, MLA_HEADS * MLA_V)


def encoder_layer(x, cos, sin, ffn1_norm, w_ffn1_in, w_ffn1_out, mix_norm, w_in,
                  gdn_conv, gdn_A_log, gdn_dt_bias, gdn_norm,
                  ssd_conv, ssd_conv_b, ssd_A_log, ssd_dt_bias, ssd_D, ssd_norm,
                  mla_q_norm, mla_w_uq, mla_kv_norm, mla_w_ukv,
                  w_branch_a, w_branch_b, w_branch_c, w_gate, b_gate, w_out,
                  ffn2_norm, w_ffn2_in, w_ffn2_out):
    h = x + 0.5 * swiglu(rmsnorm(x, ffn1_norm), w_ffn1_in, w_ffn1_out)
    u = rmsnorm(h, mix_norm)
    cols = u @ w_in
    c_a = cols[..., :GDN_COLS]
    c_b = cols[..., GDN_COLS:GDN_COLS + SSD_COLS]
    c_c = cols[..., GDN_COLS + SSD_COLS:]
    y_a = gdn_branch(c_a, gdn_conv, gdn_A_log, gdn_dt_bias, gdn_norm) @ w_branch_a
    y_b = ssd_branch(c_b, ssd_conv, ssd_conv_b, ssd_A_log, ssd_dt_bias, ssd_D, ssd_norm) @ w_branch_b
    y_c = mla_branch(c_c, cos, sin, mla_q_norm, mla_w_uq, mla_kv_norm, mla_w_ukv) @ w_branch_c
    gates = jax.nn.sigmoid((u @ w_gate + b_gate).astype(F32)).astype(u.dtype)
    g_a, g_b, g_c = jnp.split(gates, N_BRANCH, axis=-1)
    h = h + (g_a * y_a + g_b * y_b + g_c * y_c) @ w_out
    return h + 0.5 * swiglu(rmsnorm(h, ffn2_norm), w_ffn2_in, w_ffn2_out)


def _a_log(k, shape):
    return jnp.log(jax.random.uniform(k, shape, F32, minval=1.0, maxval=16.0))


def _dt_bias(k, shape):
    r = jax.random.uniform(k, shape, F32)
    dt = jnp.exp(r * (math.log(DT_MAX) - math.log(DT_MIN)) + math.log(DT_MIN))
    return dt + jnp.log(-jnp.expm1(-dt))


def setup_inputs(seed: int = 0) -> dict:
    key = jax.random.key(seed)
    ks = iter(jax.random.split(key, 40))
    nrm = lambda shape, scale: jax.random.normal(next(ks), shape, F32) * scale
    gain = lambda shape: 1.0 + 0.02 * jax.random.normal(next(ks), shape, F32)
    D, F = D_MODEL, FFN_HIDDEN
    return {
        "x_prompt": nrm((BATCH, SEQ, D), 1.0),
        "x_sample": nrm((DEC_BATCH, DEC_SEQ, D), 1.0),
        "ffn1_norm": gain((DEPTH, D)),
        "w_ffn1_in": nrm((DEPTH, D, 2 * F), D ** -0.5),
        "w_ffn1_out": nrm((DEPTH, F, D), F ** -0.5),
        "mix_norm": gain((DEPTH, D)),
        "w_in": nrm((DEPTH, D, IN_COLS), D ** -0.5),
        "gdn_conv": nrm((DEPTH, CONV_W, GDN_QKV_W), CONV_W ** -0.5),
        "gdn_A_log": _a_log(next(ks), (DEPTH, 2, GDN_HEADS)),
        "gdn_dt_bias": _dt_bias(next(ks), (DEPTH, 2, GDN_HEADS)),
        "gdn_norm": gain((DEPTH, GDN_DV)),
        "ssd_conv": nrm((DEPTH, CONV_W, SSD_XBC), CONV_W ** -0.5),
        "ssd_conv_b": nrm((DEPTH, SSD_XBC), 0.02),
        "ssd_A_log": _a_log(next(ks), (DEPTH, 2, SSD_HEADS)),
        "ssd_dt_bias": _dt_bias(next(ks), (DEPTH, 2, SSD_HEADS)),
        "ssd_D": gain((DEPTH, SSD_HEADS)),
        "ssd_norm": gain((DEPTH, SSD_INNER)),
        "mla_q_norm": gain((DEPTH, MLA_Q_LORA)),
        "mla_w_uq": nrm((DEPTH, MLA_Q_LORA, MLA_HEADS * (MLA_NOPE + MLA_ROPE)), MLA_Q_LORA ** -0.5),
        "mla_kv_norm": gain((DEPTH, MLA_KV_LORA)),
        "mla_w_ukv": nrm((DEPTH, MLA_KV_LORA, MLA_HEADS * (MLA_NOPE + MLA_V)), MLA_KV_LORA ** -0.5),
        "w_branch_a": nrm((DEPTH, GDN_V_W, D), GDN_V_W ** -0.5),
        "w_branch_b": nrm((DEPTH, SSD_INNER, D), SSD_INNER ** -0.5),
        "w_branch_c": nrm((DEPTH, MLA_HEADS * MLA_V, D), (MLA_HEADS * MLA_V) ** -0.5),
        "w_gate": nrm((DEPTH, D, N_BRANCH * D), D ** -0.5),
        "b_gate": nrm((DEPTH, N_BRANCH * D), 0.02),
        "w_out": nrm((DEPTH, D, D), D ** -0.5),
        "ffn2_norm": gain((DEPTH, D)),
        "w_ffn2_in": nrm((DEPTH, D, 2 * F), D ** -0.5),
        "w_ffn2_out": nrm((DEPTH, F, D), F ** -0.5),
        "final_norm": gain((D,)),
    }


def reference(x_prompt, x_sample, ffn1_norm, w_ffn1_in, w_ffn1_out, mix_norm, w_in,
              gdn_conv, gdn_A_log, gdn_dt_bias, gdn_norm,
              ssd_conv, ssd_conv_b, ssd_A_log, ssd_dt_bias, ssd_D, ssd_norm,
              mla_q_norm, mla_w_uq, mla_kv_norm, mla_w_ukv,
              w_branch_a, w_branch_b, w_branch_c, w_gate, b_gate, w_out,
              ffn2_norm, w_ffn2_in, w_ffn2_out, final_norm):
    def trunk(x):
        cos, sin = rope_tables(x.shape[1], x.dtype)
        for i in range(DEPTH):
            x = encoder_layer(
                x, cos, sin, ffn1_norm[i], w_ffn1_in[i], w_ffn1_out[i], mix_norm[i], w_in[i],
                gdn_conv[i], gdn_A_log[i], gdn_dt_bias[i], gdn_norm[i],
                ssd_conv[i], ssd_conv_b[i], ssd_A_log[i], ssd_dt_bias[i], ssd_D[i], ssd_norm[i],
                mla_q_norm[i], mla_w_uq[i], mla_kv_norm[i], mla_w_ukv[i],
                w_branch_a[i], w_branch_b[i], w_branch_c[i], w_gate[i], b_gate[i], w_out[i],
                ffn2_norm[i], w_ffn2_in[i], w_ffn2_out[i])
        return rmsnorm(x, final_norm)

    y_prompt = trunk(x_prompt)
    y_sample = trunk(x_sample)
    return (y_prompt, y_sample)
```

```python
import functools

import jax
import jax.numpy as jnp
from jax import lax
from jax.experimental import pallas as pl
from jax.experimental.pallas import tpu as pltpu

F32 = jnp.float32
BF16 = jnp.bfloat16

D_MODEL = 1024
DEPTH = 2
EPS = 1e-6
CONV_W = 5
CONV_HALF = CONV_W // 2
CHUNK = 64

GDN_HEADS = 4
GDN_DK = 128
GDN_DV = 128
GDN_QK_W = GDN_HEADS * GDN_DK
GDN_V_W = GDN_HEADS * GDN_DV
GDN_QKV_W = 2 * GDN_QK_W + GDN_V_W
GDN_CHAINS = 2 * GDN_HEADS

SSD_HEADS = 8
SSD_HEAD_DIM = 64
SSD_INNER = SSD_HEADS * SSD_HEAD_DIM
SSD_GROUPS = 2
SSD_STATE = 128
SSD_XBC = SSD_INNER + 2 * SSD_GROUPS * SSD_STATE
SSD_GROUP_W = SSD_INNER // SSD_GROUPS
SSD_HEADS_PER_GROUP = SSD_HEADS // SSD_GROUPS
HEAD_SHIFT = 6
assert (1 << HEAD_SHIFT) == SSD_HEAD_DIM == CHUNK

MLA_HEADS = 4
MLA_Q_LORA = 256
MLA_KV_LORA = 256
MLA_NOPE = 128
MLA_ROPE = 64
MLA_V = 128
MLA_QK_PAD = 256
MLA_SCALE = (MLA_NOPE + MLA_ROPE) ** -0.5
ROPE_BASE = 10000.0

FFN_HIDDEN = 2816
N_BRANCH = 3

BIG_QKV = 0
BIG_ZA = BIG_QKV + GDN_QKV_W
BIG_ZB = BIG_ZA + GDN_V_W
BIG_XBC = BIG_ZB + SSD_INNER
BIG_CQKV = BIG_XBC + SSD_XBC
BIG_W = BIG_CQKV + MLA_Q_LORA + MLA_KV_LORA
SMALL_W = 256
LANE = 128
SUBLANE = 8
GDN_AB_LANE = 0
SSD_DT_LANE = 16

VMEM_LIMIT = 56 * 1024 * 1024

TM_FFN = 512
TM_MIX = 512
TM_MERGE = 256
TR_SEQ = 256
TQ_ATTN = 256


def _cparams(*sem):
    return pltpu.CompilerParams(dimension_semantics=sem, vmem_limit_bytes=VMEM_LIMIT)


def _const_spec(shape):
    nd = len(shape)
    return pl.BlockSpec(shape, lambda *_: (0,) * nd, pipeline_mode=pl.Buffered(1))


def _rms(x, gain):
    return x * lax.rsqrt(jnp.mean(x * x, axis=-1, keepdims=True) + EPS) * gain


def _silu(x):
    return x * jax.nn.sigmoid(x)


def _softplus(x):
    return jnp.maximum(x, 0.0) + jnp.log1p(jnp.exp(-jnp.abs(x)))


def _dot(a, b):
    return jnp.dot(a, b, preferred_element_type=F32)


def _dot_nt(a, b):
    return lax.dot_general(a, b, (((1,), (1,)), ((), ())), preferred_element_type=F32)


def _dot_tn(a, b):
    return lax.dot_general(a, b, (((0,), (0,)), ((), ())), preferred_element_type=F32)


def _ffn_kernel(x_ref, g_ref, win_ref, wout_ref, fin_ref, o_ref, *, final):
    x = x_ref[...]
    xn = _rms(x, g_ref[...]).astype(BF16)
    gu = _dot(xn, win_ref[...])
    act = (_silu(gu[:, :FFN_HIDDEN]) * gu[:, FFN_HIDDEN:]).astype(BF16)
    h = x + 0.5 * _dot(act, wout_ref[...])
    if final:
        h = _rms(h, fin_ref[...])
    o_ref[...] = h


def _ffn(x2d, gain, w_in, w_out, fin, final):
    T = x2d.shape[0]
    tm = TM_FFN
    row = pl.BlockSpec((tm, D_MODEL), lambda i: (i, 0))
    return pl.pallas_call(
        functools.partial(_ffn_kernel, final=final),
        grid=(T // tm,),
        in_specs=[row, _const_spec((1, D_MODEL)), _const_spec((D_MODEL, 2 * FFN_HIDDEN)),
                  _const_spec((FFN_HIDDEN, D_MODEL)), _const_spec((1, D_MODEL))],
        out_specs=row,
        out_shape=jax.ShapeDtypeStruct((T, D_MODEL), F32),
        compiler_params=_cparams("parallel"),
        name="ffn",
    )(x2d, gain, w_in, w_out, fin)


def _mix_kernel(h_ref, g_ref, wbig_ref, wsmall_ref, qkv_ref, za_ref, zb_ref, xbc_ref, cqkv_ref, small_ref):
    u = _rms(h_ref[...], g_ref[...]).astype(BF16)
    big = _dot(u, wbig_ref[...])
    qkv_ref[...] = big[:, BIG_QKV:BIG_ZA].astype(BF16)
    za_ref[...] = big[:, BIG_ZA:BIG_ZB].astype(BF16)
    zb_ref[...] = big[:, BIG_ZB:BIG_XBC].astype(BF16)
    xbc_ref[...] = big[:, BIG_XBC:BIG_CQKV].astype(BF16)
    cqkv_ref[...] = big[:, BIG_CQKV:BIG_W].astype(BF16)
    small_ref[...] = _dot(u, wsmall_ref[...])


def _mix(h2d, gain, w_big, w_small):
    T = h2d.shape[0]
    tm = TM_MIX
    row = lambda w: pl.BlockSpec((tm, w), lambda i: (i, 0))
    widths = (GDN_QKV_W, GDN_V_W, SSD_INNER, SSD_XBC, MLA_Q_LORA + MLA_KV_LORA)
    return pl.pallas_call(
        _mix_kernel,
        grid=(T // tm,),
        in_specs=[row(D_MODEL), _const_spec((1, D_MODEL)), _const_spec((D_MODEL, BIG_W)),
                  _const_spec((D_MODEL, SMALL_W))],
        out_specs=[row(w) for w in widths] + [row(SMALL_W)],
        out_shape=[jax.ShapeDtypeStruct((T, w), BF16) for w in widths]
        + [jax.ShapeDtypeStruct((T, SMALL_W), F32)],
        compiler_params=_cparams("parallel"),
        name="mix",
    )(h2d, gain, w_big, w_small)


def _seq_specs(tr, width, L):
    per = tr // SUBLANE
    last = L // SUBLANE - 1
    cur = pl.BlockSpec((1, tr, width), lambda b, i: (b, i, 0))
    prev = pl.BlockSpec((1, SUBLANE, width), lambda b, i: (b, jnp.maximum(i * per - 1, 0), 0))
    nxt = pl.BlockSpec((1, SUBLANE, width), lambda b, i: (b, jnp.minimum((i + 1) * per, last), 0))
    return cur, prev, nxt


def _conv_tile(cur_ref, prev_ref, next_ref, w_ref):
    i = pl.program_id(1)
    tr = cur_ref.shape[1]
    x = cur_ref[0].astype(F32)
    prev = jnp.where(i > 0, prev_ref[0].astype(F32), 0.0)
    nxt = jnp.where(i < pl.num_programs(1) - 1, next_ref[0].astype(F32), 0.0)
    xe = jnp.concatenate([prev, x, nxt], axis=0)
    n = tr + 2 * SUBLANE
    acc = None
    for k in range(CONV_W):
        shifted = xe if k == CONV_HALF else pltpu.roll(xe, (CONV_HALF - k) % n, 0)
        term = shifted[SUBLANE:SUBLANE + tr] * w_ref[k:k + 1, :]
        acc = term if acc is None else acc + term
    return acc


def _chunk_cumsum(g, lane_is_fwd):
    tr = g.shape[0]
    pos = lax.broadcasted_iota(jnp.int32, g.shape, 0) & (CHUNK - 1)
    pre = g
    suf = g
    s = 1
    while s < CHUNK:
        pre = pre + jnp.where(pos >= s, pltpu.roll(pre, s, 0), 0.0)
        suf = suf + jnp.where(pos < CHUNK - s, pltpu.roll(suf, tr - s, 0), 0.0)
        s *= 2
    return jnp.where(lane_is_fwd, pre, suf)


def _gdn_prep_kernel(cur_ref, prev_ref, next_ref, small_ref, w_ref, alog_ref, bias_ref,
                     q_ref, k_ref, v_ref, gb_ref):
    qkv = _silu(_conv_tile(cur_ref, prev_ref, next_ref, w_ref))
    for h in range(GDN_HEADS):
        sl = slice(h * GDN_DK, (h + 1) * GDN_DK)
        q = qkv[:, sl]
        q_ref[0, :, sl] = (q * lax.rsqrt(jnp.sum(q * q, axis=-1, keepdims=True) + EPS)
                           * GDN_DK ** -0.5).astype(BF16)
        k = qkv[:, GDN_QK_W + h * GDN_DK:GDN_QK_W + (h + 1) * GDN_DK]
        k_ref[0, :, sl] = (k * lax.rsqrt(jnp.sum(k * k, axis=-1, keepdims=True) + EPS)).astype(BF16)
    v_ref[0] = qkv[:, 2 * GDN_QK_W:].astype(BF16)

    s = small_ref[0]
    lane = lax.broadcasted_iota(jnp.int32, s.shape, 1)
    g = jnp.where(lane < GDN_CHAINS, -jnp.exp(alog_ref[...]) * _softplus(s + bias_ref[...]), 0.0)
    gc = _chunk_cumsum(g, lane < GDN_HEADS)
    gb_ref[0] = jnp.where(lane < GDN_CHAINS, gc, jax.nn.sigmoid(s))


def _gdn_prep(qkv, small, conv_w, alog_row, bias_row):
    B, L, _ = qkv.shape
    tr = TR_SEQ
    cur, prev, nxt = _seq_specs(tr, GDN_QKV_W, L)
    out = lambda w: pl.BlockSpec((1, tr, w), lambda b, i: (b, i, 0))
    return pl.pallas_call(
        _gdn_prep_kernel,
        grid=(B, L // tr),
        in_specs=[cur, prev, nxt, pl.BlockSpec((1, tr, LANE), lambda b, i: (b, i, 0)),
                  _const_spec((SUBLANE, GDN_QKV_W)), _const_spec((1, LANE)), _const_spec((1, LANE))],
        out_specs=[out(GDN_QK_W), out(GDN_QK_W), out(GDN_V_W), out(LANE)],
        out_shape=[jax.ShapeDtypeStruct((B, L, GDN_QK_W), BF16), jax.ShapeDtypeStruct((B, L, GDN_QK_W), BF16),
                   jax.ShapeDtypeStruct((B, L, GDN_V_W), BF16), jax.ShapeDtypeStruct((B, L, LANE), F32)],
        compiler_params=_cparams("parallel", "parallel"),
        name="gdn_prep",
    )(qkv, qkv, qkv, small, conv_w, alog_row, bias_row)


NEUMANN_STEPS = 6
assert 2 ** NEUMANN_STEPS == CHUNK


def _gdn_chain(d, q, k, v, gcol, grow, bcol, s_ref, j):
    ii = lax.broadcasted_iota(jnp.int32, (CHUNK, CHUNK), 0)
    jj = lax.broadcasted_iota(jnp.int32, (CHUNK, CHUNK), 1)
    incl = (ii >= jj) if d == 0 else (ii <= jj)
    dec = jnp.where(incl, jnp.exp(jnp.minimum(gcol - grow, 0.0)), 0.0)
    kk = _dot_nt(k, k)
    qk = _dot_nt(q, k)
    nmat = -(bcol * kk) * jnp.where(ii == jj, 0.0, dec)
    attn = (qk * dec).astype(BF16)
    kf = k.astype(F32)
    eg = jnp.exp(gcol)
    glast = gcol[CHUNK - 1:CHUNK] if d == 0 else gcol[0:1]
    x = bcol * jnp.concatenate([v.astype(F32), kf * eg], axis=1)
    for step in range(NEUMANN_STEPS):
        nb = nmat.astype(BF16)
        x = x + _dot(nb, x.astype(BF16))
        if step + 1 < NEUMANN_STEPS:
            nmat = _dot(nb, nb)
    u = x[:, :GDN_DV]
    w = x[:, GDN_DV:]
    qd = q.astype(F32) * eg
    kd = (kf * jnp.exp(glast - gcol)).astype(BF16)
    s = s_ref[j]
    r = _dot(jnp.concatenate([w, qd], axis=0).astype(BF16), s.astype(BF16))
    vnew = (u - r[:CHUNK]).astype(BF16)
    o = r[CHUNK:] + _dot(attn, vnew)
    s_ref[j] = s * jnp.exp(glast) + _dot_tn(kd, vnew)
    return o


def _gdn_scan_kernel(qf_ref, kf_ref, vf_ref, gbf_ref, grf_ref, qb_ref, kb_ref, vb_ref, gbb_ref, grb_ref,
                     of_ref, ob_ref, s_ref, *, nc):
    @pl.when(pl.program_id(1) == 0)
    def _():
        s_ref[...] = jnp.zeros_like(s_ref)

    dirs = ((qf_ref, kf_ref, vf_ref, gbf_ref, grf_ref, of_ref),
            (qb_ref, kb_ref, vb_ref, gbb_ref, grb_ref, ob_ref))

    def chunk(c, carry):
        for d, (q_ref, k_ref, v_ref, gb_ref, gr_ref, o_ref) in enumerate(dirs):
            cc = c if d == 0 else nc - 1 - c
            rows = pl.ds(pl.multiple_of(cc * CHUNK, CHUNK), CHUNK)
            gb = gb_ref[0, rows, :]
            grow_all = gr_ref[0, cc]
            for h in range(GDN_HEADS):
                j = d * GDN_HEADS + h
                sl = slice(h * GDN_DK, (h + 1) * GDN_DK)
                o = _gdn_chain(d, q_ref[0, rows, sl], k_ref[0, rows, sl], v_ref[0, rows, sl],
                               gb[:, j:j + 1], grow_all[j:j + 1, :],
                               gb[:, GDN_CHAINS + j:GDN_CHAINS + j + 1], s_ref, j)
                o_ref[0, rows, sl] = o
        return carry

    lax.fori_loop(0, nc, chunk, 0)


def _gdn_scan(q, k, v, gb, grow):
    B, L, _ = q.shape
    tr = TR_SEQ
    nc = tr // CHUNK
    nt = L // tr
    fwd = lambda w: pl.BlockSpec((1, tr, w), lambda b, i: (b, i, 0))
    bwd = lambda w: pl.BlockSpec((1, tr, w), lambda b, i: (b, nt - 1 - i, 0))
    grf = pl.BlockSpec((1, nc, GDN_CHAINS, CHUNK), lambda b, i: (b, i, 0, 0))
    grb = pl.BlockSpec((1, nc, GDN_CHAINS, CHUNK), lambda b, i: (b, nt - 1 - i, 0, 0))
    return pl.pallas_call(
        functools.partial(_gdn_scan_kernel, nc=nc),
        grid=(B, nt),
        in_specs=[fwd(GDN_QK_W), fwd(GDN_QK_W), fwd(GDN_V_W), fwd(LANE), grf,
                  bwd(GDN_QK_W), bwd(GDN_QK_W), bwd(GDN_V_W), bwd(LANE), grb],
        out_specs=[fwd(GDN_V_W), bwd(GDN_V_W)],
        out_shape=[jax.ShapeDtypeStruct((B, L, GDN_V_W), F32)] * 2,
        scratch_shapes=[pltpu.VMEM((GDN_CHAINS, GDN_DK, GDN_DV), F32)],
        compiler_params=_cparams("arbitrary", "arbitrary"),
        name="gdn_scan",
    )(q, k, v, gb, grow, q, k, v, gb, grow)


def _ssd_prep_kernel(cur_ref, prev_ref, next_ref, small_ref, w_ref, cb_ref, alog_ref, bias_ref,
                     xbc_ref, dt_ref, ac_ref):
    xbc_ref[0] = _silu(_conv_tile(cur_ref, prev_ref, next_ref, w_ref) + cb_ref[...]).astype(BF16)
    s = small_ref[0]
    lane = lax.broadcasted_iota(jnp.int32, s.shape, 1)
    live = (lane >= SSD_DT_LANE) & (lane < SSD_DT_LANE + 2 * SSD_HEADS)
    dt = jnp.where(live, _softplus(s + bias_ref[...]), 0.0)
    dt_ref[0] = dt
    ac_ref[0] = _chunk_cumsum(dt * -jnp.exp(alog_ref[...]), lane < SSD_DT_LANE + SSD_HEADS)


def _ssd_prep(xbc, small, conv_w, conv_b, alog_row, bias_row):
    B, L, _ = xbc.shape
    tr = TR_SEQ
    cur, prev, nxt = _seq_specs(tr, SSD_XBC, L)
    out = lambda w: pl.BlockSpec((1, tr, w), lambda b, i: (b, i, 0))
    return pl.pallas_call(
        _ssd_prep_kernel,
        grid=(B, L // tr),
        in_specs=[cur, prev, nxt, pl.BlockSpec((1, tr, LANE), lambda b, i: (b, i, 0)),
                  _const_spec((SUBLANE, SSD_XBC)), _const_spec((1, SSD_XBC)),
                  _const_spec((1, LANE)), _const_spec((1, LANE))],
        out_specs=[out(SSD_XBC), out(LANE), out(LANE)],
        out_shape=[jax.ShapeDtypeStruct((B, L, SSD_XBC), BF16), jax.ShapeDtypeStruct((B, L, LANE), F32),
                   jax.ShapeDtypeStruct((B, L, LANE), F32)],
        compiler_params=_cparams("parallel", "parallel"),
        name="ssd_prep",
    )(xbc, xbc, xbc, small, conv_w, conv_b, alog_row, bias_row)


def _expand_heads(cols, base):
    rows = cols.shape[0]
    head = lax.broadcasted_iota(jnp.int32, (rows, SSD_INNER), 1) >> HEAD_SHIFT
    out = jnp.zeros((rows, SSD_INNER), F32)
    for h in range(SSD_HEADS):
        out = jnp.where(head == h, cols[:, base + h:base + h + 1], out)
    return out


def _ssd_chunk(d, xbc, dtc, acc, acrow, h_ref):
    base = SSD_DT_LANE + d * SSD_HEADS
    dtx = _expand_heads(dtc, base)
    acx = _expand_heads(acc, base)
    xr = xbc[:, :SSD_INNER].astype(F32) * dtx
    aclast = acx[CHUNK - 1:CHUNK] if d == 0 else acx[0:1]
    ii = lax.broadcasted_iota(jnp.int32, (CHUNK, SSD_GROUP_W), 0)
    jj = lax.broadcasted_iota(jnp.int32, (CHUNK, SSD_GROUP_W), 1) & (CHUNK - 1)
    incl = (ii >= jj) if d == 0 else (ii <= jj)
    rb = lax.broadcasted_iota(jnp.int32, (SSD_GROUP_W, SSD_GROUP_W), 0) >> HEAD_SHIFT
    cb_ = lax.broadcasted_iota(jnp.int32, (SSD_GROUP_W, SSD_GROUP_W), 1) >> HEAD_SHIFT
    outs = []
    for g in range(SSD_GROUPS):
        sl = slice(g * SSD_GROUP_W, (g + 1) * SSD_GROUP_W)
        bg = xbc[:, SSD_INNER + g * SSD_STATE:SSD_INNER + (g + 1) * SSD_STATE]
        cg = xbc[:, SSD_INNER + (SSD_GROUPS + g) * SSD_STATE:SSD_INNER + (SSD_GROUPS + g + 1) * SSD_STATE]
        cbt = _dot_nt(cg, jnp.concatenate([bg] * SSD_HEADS_PER_GROUP, axis=0))
        seg = jnp.where(incl, jnp.exp(jnp.minimum(acx[:, sl] - acrow[:, sl], 0.0)), 0.0)
        m = (cbt * seg).astype(BF16)
        xr_g = xr[:, sl]
        xr_bd = jnp.where(rb == cb_, jnp.concatenate([xr_g] * SSD_HEADS_PER_GROUP, axis=0), 0.0).astype(BF16)
        hg = h_ref[d, g]
        y = _dot(m, xr_bd) + _dot(cg, hg.astype(BF16)) * jnp.exp(acx[:, sl])
        outs.append(y)
        decay_states = jnp.exp(aclast[:, sl] - acx[:, sl])
        h_ref[d, g] = hg * jnp.exp(aclast[:, sl]) + _dot_tn(bg, (xr_g * decay_states).astype(BF16))
    return jnp.concatenate(outs, axis=1)


def _ssd_scan_kernel(xf_ref, dtf_ref, acf_ref, arf_ref, xb_ref, dtb_ref, acb_ref, arb_ref,
                     yf_ref, yb_ref, h_ref, *, nc):
    @pl.when(pl.program_id(1) == 0)
    def _():
        h_ref[...] = jnp.zeros_like(h_ref)

    dirs = ((xf_ref, dtf_ref, acf_ref, arf_ref, yf_ref), (xb_ref, dtb_ref, acb_ref, arb_ref, yb_ref))

    def chunk(c, carry):
        for d, (x_ref, dt_ref, ac_ref, ar_ref, y_ref) in enumerate(dirs):
            cc = c if d == 0 else nc - 1 - c
            rows = pl.ds(pl.multiple_of(cc * CHUNK, CHUNK), CHUNK)
            y_ref[0, rows, :] = _ssd_chunk(d, x_ref[0, rows, :], dt_ref[0, rows, :], ac_ref[0, rows, :],
                                           ar_ref[0, cc], h_ref)
        return carry

    lax.fori_loop(0, nc, chunk, 0)


def _ssd_scan(xbc, dt, ac, acrow_f, acrow_b):
    B, L, _ = xbc.shape
    tr = TR_SEQ
    nc = tr // CHUNK
    nt = L // tr
    fwd = lambda w: pl.BlockSpec((1, tr, w), lambda b, i: (b, i, 0))
    bwd = lambda w: pl.BlockSpec((1, tr, w), lambda b, i: (b, nt - 1 - i, 0))
    arf = pl.BlockSpec((1, nc, 1, SSD_INNER), lambda b, i: (b, i, 0, 0))
    arb = pl.BlockSpec((1, nc, 1, SSD_INNER), lambda b, i: (b, nt - 1 - i, 0, 0))
    return pl.pallas_call(
        functools.partial(_ssd_scan_kernel, nc=nc),
        grid=(B, nt),
        in_specs=[fwd(SSD_XBC), fwd(LANE), fwd(LANE), arf, bwd(SSD_XBC), bwd(LANE), bwd(LANE), arb],
        out_specs=[fwd(SSD_INNER), bwd(SSD_INNER)],
        out_shape=[jax.ShapeDtypeStruct((B, L, SSD_INNER), F32)] * 2,
        scratch_shapes=[pltpu.VMEM((2, SSD_GROUPS, SSD_STATE, SSD_GROUP_W), F32)],
        compiler_params=_cparams("arbitrary", "arbitrary"),
        name="ssd_scan",
    )(xbc, dt, ac, acrow_f, xbc, dt, ac, acrow_b)


def _rope_pair(pair, cs):
    prod = pair * cs
    lane = lax.broadcasted_iota(jnp.int32, prod.shape, 1)
    return jnp.where(lane < MLA_ROPE, prod + pltpu.roll(prod, MLA_ROPE, 1), 0.0)


def _mla_prep_kernel(c_ref, small_ref, cs_ref, qn_ref, kvn_ref, wq_ref, wkv_ref, q_ref, k_ref, v_ref):
    c = c_ref[0].astype(F32)
    cs = cs_ref[...]
    cq = _rms(c[:, :MLA_Q_LORA], qn_ref[...]).astype(BF16)
    ckv = _rms(c[:, MLA_Q_LORA:], kvn_ref[...]).astype(BF16)
    qa = _dot(cq, wq_ref[...]) * MLA_SCALE
    kva = _dot(ckv, wkv_ref[...])
    k_rope = _rope_pair(small_ref[0], cs).astype(BF16)
    for h in range(MLA_HEADS):
        base = h * MLA_QK_PAD
        q_ref[0, h, :, :MLA_NOPE] = qa[:, base:base + MLA_NOPE].astype(BF16)
        q_ref[0, h, :, MLA_NOPE:] = _rope_pair(qa[:, base + MLA_NOPE:base + MLA_QK_PAD], cs).astype(BF16)
        k_ref[0, h, :, :MLA_NOPE] = kva[:, base:base + MLA_NOPE].astype(BF16)
        k_ref[0, h, :, MLA_NOPE:] = k_rope
        v_ref[0, h] = kva[:, base + MLA_NOPE:base + MLA_NOPE + MLA_V].astype(BF16)


def _mla_prep(cqkv, small, cs, q_norm, kv_norm, wq, wkv):
    B, L, _ = cqkv.shape
    tr = TR_SEQ
    head_out = lambda w: pl.BlockSpec((1, MLA_HEADS, tr, w), lambda b, i: (b, 0, i, 0))
    return pl.pallas_call(
        _mla_prep_kernel,
        grid=(B, L // tr),
        in_specs=[pl.BlockSpec((1, tr, MLA_Q_LORA + MLA_KV_LORA), lambda b, i: (b, i, 0)),
                  pl.BlockSpec((1, tr, LANE), lambda b, i: (b, i, 1)),
                  pl.BlockSpec((tr, LANE), lambda b, i: (i, 0)),
                  _const_spec((1, MLA_Q_LORA)), _const_spec((1, MLA_KV_LORA)),
                  _const_spec((MLA_Q_LORA, MLA_HEADS * MLA_QK_PAD)),
                  _const_spec((MLA_KV_LORA, MLA_HEADS * (MLA_NOPE + MLA_V)))],
        out_specs=[head_out(MLA_QK_PAD), head_out(MLA_QK_PAD), head_out(MLA_V)],
        out_shape=[jax.ShapeDtypeStruct((B, MLA_HEADS, L, MLA_QK_PAD), BF16),
                   jax.ShapeDtypeStruct((B, MLA_HEADS, L, MLA_QK_PAD), BF16),
                   jax.ShapeDtypeStruct((B, MLA_HEADS, L, MLA_V), BF16)],
        compiler_params=_cparams("parallel", "parallel"),
        name="mla_prep",
    )(cqkv, small, cs, q_norm, kv_norm, wq, wkv)


def _mla_attn_kernel(q_ref, k_ref, v_ref, o_ref):
    s = _dot_nt(q_ref[0, 0], k_ref[0, 0])
    p = jnp.exp(s - jnp.max(s, axis=-1, keepdims=True))
    denom = jnp.sum(p, axis=-1, keepdims=True)
    o_ref[0] = (_dot(p.astype(BF16), v_ref[0, 0]) / denom).astype(BF16)


def _mla_attn(q, k, v):
    B, H, L, _ = q.shape
    tq = TQ_ATTN
    return pl.pallas_call(
        _mla_attn_kernel,
        grid=(B, H, L // tq),
        in_specs=[pl.BlockSpec((1, 1, tq, MLA_QK_PAD), lambda b, h, i: (b, h, i, 0)),
                  pl.BlockSpec((1, 1, L, MLA_QK_PAD), lambda b, h, i: (b, h, 0, 0)),
                  pl.BlockSpec((1, 1, L, MLA_V), lambda b, h, i: (b, h, 0, 0))],
        out_specs=pl.BlockSpec((1, tq, MLA_V), lambda b, h, i: (b, i, h)),
        out_shape=jax.ShapeDtypeStruct((B, L, H * MLA_V), BF16),
        compiler_params=_cparams("parallel", "parallel", "parallel"),
        name="mla_attn",
    )(q, k, v)


def _merge_kernel(h_ref, of_ref, ob_ref, za_ref, yf_ref, yb_ref, xbc_ref, zb_ref, oc_ref,
                  mixg_ref, gng_ref, dskip_ref, sng_ref, wa_ref, wb_ref, wc_ref, wg_ref, bg_ref, wo_ref,
                  out_ref):
    h = h_ref[...]
    u = _rms(h, mixg_ref[...]).astype(BF16)
    gates = jax.nn.sigmoid(_dot(u, wg_ref[...]) + bg_ref[...])

    o = of_ref[...] + ob_ref[...]
    za = za_ref[...].astype(F32)
    parts = []
    for hd in range(GDN_HEADS):
        sl = slice(hd * GDN_DV, (hd + 1) * GDN_DV)
        parts.append(_rms(o[:, sl], gng_ref[...]) * _silu(za[:, sl]))
    ya = _dot(jnp.concatenate(parts, axis=1).astype(BF16), wa_ref[...])

    y = yf_ref[...] + yb_ref[...] + xbc_ref[...].astype(F32) * dskip_ref[...]
    y = _rms(y * _silu(zb_ref[...].astype(F32)), sng_ref[...])
    yb = _dot(y.astype(BF16), wb_ref[...])

    yc = _dot(oc_ref[...], wc_ref[...])

    merged = (gates[:, :D_MODEL] * ya + gates[:, D_MODEL:2 * D_MODEL] * yb
              + gates[:, 2 * D_MODEL:] * yc)
    out_ref[...] = h + _dot(merged.astype(BF16), wo_ref[...])


def _merge(h2d, o_f, o_b, za, y_f, y_b, xbc_act, zb, oc, mix_gain, gdn_gain, dskip, ssd_gain,
           wa, wb, wc, wg, bg, wo):
    T = h2d.shape[0]
    tm = TM_MERGE
    row = lambda w: pl.BlockSpec((tm, w), lambda i: (i, 0))
    return pl.pallas_call(
        _merge_kernel,
        grid=(T // tm,),
        in_specs=[row(D_MODEL), row(GDN_V_W), row(GDN_V_W), row(GDN_V_W), row(SSD_INNER), row(SSD_INNER),
                  row(SSD_INNER), row(SSD_INNER), row(MLA_HEADS * MLA_V),
                  _const_spec((1, D_MODEL)), _const_spec((1, GDN_DV)), _const_spec((1, SSD_INNER)),
                  _const_spec((1, SSD_INNER)), _const_spec((GDN_V_W, D_MODEL)),
                  _const_spec((SSD_INNER, D_MODEL)), _const_spec((MLA_HEADS * MLA_V, D_MODEL)),
                  _const_spec((D_MODEL, N_BRANCH * D_MODEL)), _const_spec((1, N_BRANCH * D_MODEL)),
                  _const_spec((D_MODEL, D_MODEL))],
        out_specs=row(D_MODEL),
        out_shape=jax.ShapeDtypeStruct((T, D_MODEL), F32),
        compiler_params=_cparams("parallel"),
        name="merge",
    )(h2d, o_f, o_b, za, y_f, y_b, xbc_act, zb, oc, mix_gain, gdn_gain, dskip, ssd_gain,
      wa, wb, wc, wg, bg, wo)


def _rotate_half_cols(w):
    w1, w2 = jnp.split(w, 2, axis=-1)
    return jnp.concatenate([-w2, w1], axis=-1)


def _lane_row(values, base):
    row = jnp.zeros((1, LANE), F32)
    return row.at[0, base:base + values.shape[0]].set(values.astype(F32))


def _prep_layer(p, i):
    w_in = p["w_in"][i]
    c = 0
    seg = {}
    for name, width in (("qkv", GDN_QKV_W), ("za", GDN_V_W), ("ab", 4 * GDN_HEADS), ("zb", SSD_INNER),
                        ("xbc", SSD_XBC), ("dt", 2 * SSD_HEADS), ("cq", MLA_Q_LORA), ("ckv", MLA_KV_LORA),
                        ("kr", MLA_ROPE)):
        seg[name] = w_in[:, c:c + width]
        c += width
    w_big = jnp.concatenate([seg["qkv"], seg["za"], seg["zb"], seg["xbc"], seg["cq"], seg["ckv"]], axis=1)
    pad = jnp.zeros((D_MODEL, LANE - 4 * GDN_HEADS - 2 * SSD_HEADS), F32)
    w_small = jnp.concatenate([seg["ab"], seg["dt"], pad, seg["kr"], _rotate_half_cols(seg["kr"])], axis=1)

    wq = p["mla_w_uq"][i].reshape(MLA_Q_LORA, MLA_HEADS, MLA_NOPE + MLA_ROPE)
    wq_rope = wq[..., MLA_NOPE:]
    wq = jnp.concatenate([wq, _rotate_half_cols(wq_rope)], axis=-1).reshape(MLA_Q_LORA, MLA_HEADS * MLA_QK_PAD)

    row = lambda v: v.reshape(1, -1).astype(F32)
    conv_pad = lambda w: jnp.concatenate([w, jnp.zeros((SUBLANE - CONV_W, w.shape[1]), F32)], axis=0)
    return dict(
        ffn1_norm=row(p["ffn1_norm"][i]), w_ffn1_in=p["w_ffn1_in"][i].astype(BF16),
        w_ffn1_out=p["w_ffn1_out"][i].astype(BF16),
        mix_norm=row(p["mix_norm"][i]), w_big=w_big.astype(BF16), w_small=w_small.astype(BF16),
        gdn_conv=conv_pad(p["gdn_conv"][i]),
        gdn_alog=_lane_row(p["gdn_A_log"][i].reshape(-1), GDN_AB_LANE),
        gdn_bias=_lane_row(p["gdn_dt_bias"][i].reshape(-1), GDN_AB_LANE),
        gdn_norm=row(p["gdn_norm"][i]),
        ssd_conv=conv_pad(p["ssd_conv"][i]), ssd_conv_b=row(p["ssd_conv_b"][i]),
        ssd_alog=_lane_row(p["ssd_A_log"][i].reshape(-1), SSD_DT_LANE),
        ssd_bias=_lane_row(p["ssd_dt_bias"][i].reshape(-1), SSD_DT_LANE),
        ssd_dskip=row(jnp.repeat(p["ssd_D"][i], SSD_HEAD_DIM)), ssd_norm=row(p["ssd_norm"][i]),
        mla_q_norm=row(p["mla_q_norm"][i]), mla_kv_norm=row(p["mla_kv_norm"][i]),
        mla_wq=wq.astype(BF16), mla_wkv=p["mla_w_ukv"][i].astype(BF16),
        w_branch_a=p["w_branch_a"][i].astype(BF16), w_branch_b=p["w_branch_b"][i].astype(BF16),
        w_branch_c=p["w_branch_c"][i].astype(BF16), w_gate=p["w_gate"][i].astype(BF16),
        b_gate=row(p["b_gate"][i]), w_out=p["w_out"][i].astype(BF16),
        ffn2_norm=row(p["ffn2_norm"][i]), w_ffn2_in=p["w_ffn2_in"][i].astype(BF16),
        w_ffn2_out=p["w_ffn2_out"][i].astype(BF16),
    )


def _rope_table(L):
    inv_freq = jnp.power(ROPE_BASE, -jnp.arange(0, MLA_ROPE, 2, dtype=F32) / MLA_ROPE)
    ang = jnp.arange(L, dtype=F32)[:, None] * inv_freq[None, :]
    ang = jnp.concatenate([ang, ang], axis=-1)
    return jnp.concatenate([jnp.cos(ang), jnp.sin(ang)], axis=-1)


def _rows_to_chunk_rows(cols, B, L):
    n = cols.shape[-1]
    return cols.reshape(B, L // CHUNK, CHUNK, n).transpose(0, 1, 3, 2)


def _layer(x2d, B, L, cs, lp, final_gain, final):
    T = B * L
    h = _ffn(x2d, lp["ffn1_norm"], lp["w_ffn1_in"], lp["w_ffn1_out"], final_gain, False)
    qkv, za, zb, xbc, cqkv, small = _mix(h, lp["mix_norm"], lp["w_big"], lp["w_small"])
    seq = lambda a: a.reshape(B, L, a.shape[-1])
    small3 = seq(small)

    q, k, v, gb = _gdn_prep(seq(qkv), small3, lp["gdn_conv"], lp["gdn_alog"], lp["gdn_bias"])
    grow = _rows_to_chunk_rows(gb[..., :GDN_CHAINS], B, L)
    o_f, o_b = _gdn_scan(q, k, v, gb, grow)

    xbc_act, dt, ac = _ssd_prep(seq(xbc), small3, lp["ssd_conv"], lp["ssd_conv_b"], lp["ssd_alog"],
                                lp["ssd_bias"])
    acrow = _rows_to_chunk_rows(ac[..., SSD_DT_LANE:SSD_DT_LANE + 2 * SSD_HEADS], B, L)
    acrow = acrow.reshape(B, L // CHUNK, 2, 1, SSD_HEADS, 1, CHUNK)
    acrow = jnp.broadcast_to(acrow, (B, L // CHUNK, 2, 1, SSD_HEADS, SSD_HEAD_DIM // CHUNK, CHUNK))
    acrow = acrow.reshape(B, L // CHUNK, 2, 1, SSD_INNER)
    y_f, y_b = _ssd_scan(xbc_act, dt, ac, acrow[:, :, 0], acrow[:, :, 1])

    qc, kc, vc = _mla_prep(seq(cqkv), small3, cs, lp["mla_q_norm"], lp["mla_kv_norm"], lp["mla_wq"],
                           lp["mla_wkv"])
    oc = _mla_attn(qc, kc, vc)

    flat = lambda a: a.reshape(T, a.shape[-1])
    h = _merge(h, flat(o_f), flat(o_b), za, flat(y_f), flat(y_b), flat(xbc_act), zb, flat(oc),
               lp["mix_norm"], lp["gdn_norm"], lp["ssd_dskip"], lp["ssd_norm"],
               lp["w_branch_a"], lp["w_branch_b"], lp["w_branch_c"], lp["w_gate"], lp["b_gate"], lp["w_out"])
    return _ffn(h, lp["ffn2_norm"], lp["w_ffn2_in"], lp["w_ffn2_out"], final_gain, final)


def _trunk(x, layers, final_gain):
    B, L, _ = x.shape
    cs = _rope_table(L)
    h = x.reshape(B * L, D_MODEL)
    for i, lp in enumerate(layers):
        h = _layer(h, B, L, cs, lp, final_gain, i == len(layers) - 1)
    return h.reshape(B, L, D_MODEL)


def kernel(x_prompt, x_sample, ffn1_norm, w_ffn1_in, w_ffn1_out, mix_norm, w_in, gdn_conv, gdn_A_log, gdn_dt_bias, gdn_norm, ssd_conv, ssd_conv_b, ssd_A_log, ssd_dt_bias, ssd_D, ssd_norm, mla_q_norm, mla_w_uq, mla_kv_norm, mla_w_ukv, w_branch_a, w_branch_b, w_branch_c, w_gate, b_gate, w_out, ffn2_norm, w_ffn2_in, w_ffn2_out, final_norm):
    p = dict(ffn1_norm=ffn1_norm, w_ffn1_in=w_ffn1_in, w_ffn1_out=w_ffn1_out, mix_norm=mix_norm, w_in=w_in,
             gdn_conv=gdn_conv, gdn_A_log=gdn_A_log, gdn_dt_bias=gdn_dt_bias, gdn_norm=gdn_norm,
             ssd_conv=ssd_conv, ssd_conv_b=ssd_conv_b, ssd_A_log=ssd_A_log, ssd_dt_bias=ssd_dt_bias,
             ssd_D=ssd_D, ssd_norm=ssd_norm, mla_q_norm=mla_q_norm, mla_w_uq=mla_w_uq,
             mla_kv_norm=mla_kv_norm, mla_w_ukv=mla_w_ukv, w_branch_a=w_branch_a, w_branch_b=w_branch_b,
             w_branch_c=w_branch_c, w_gate=w_gate, b_gate=b_gate, w_out=w_out, ffn2_norm=ffn2_norm,
             w_ffn2_in=w_ffn2_in, w_ffn2_out=w_ffn2_out)
    layers = [_prep_layer(p, i) for i in range(DEPTH)]
    final_gain = final_norm.reshape(1, D_MODEL).astype(F32)
    return (_trunk(x_prompt, layers, final_gain), _trunk(x_sample, layers, final_gain))
```

```python
import functools

import jax
import jax.numpy as jnp
from jax import lax
from jax.experimental import pallas as pl
from jax.experimental.pallas import tpu as pltpu

F32 = jnp.float32
BF16 = jnp.bfloat16

D_MODEL = 1024
DEPTH = 2
EPS = 1e-6
CONV_W = 5
CONV_HALF = CONV_W // 2
CHUNK = 64

GDN_HEADS = 4
GDN_DK = 128
GDN_DV = 128
GDN_QK_W = GDN_HEADS * GDN_DK
GDN_V_W = GDN_HEADS * GDN_DV
GDN_QKV_W = 2 * GDN_QK_W + GDN_V_W
GDN_CHAINS = 2 * GDN_HEADS

SSD_HEADS = 8
SSD_HEAD_DIM = 64
SSD_INNER = SSD_HEADS * SSD_HEAD_DIM
SSD_GROUPS = 2
SSD_STATE = 128
SSD_XBC = SSD_INNER + 2 * SSD_GROUPS * SSD_STATE
SSD_GROUP_W = SSD_INNER // SSD_GROUPS
SSD_HEADS_PER_GROUP = SSD_HEADS // SSD_GROUPS
HEAD_SHIFT = 6
assert (1 << HEAD_SHIFT) == SSD_HEAD_DIM == CHUNK

MLA_HEADS = 4
MLA_Q_LORA = 256
MLA_KV_LORA = 256
MLA_NOPE = 128
MLA_ROPE = 64
MLA_V = 128
MLA_QK_PAD = 256
MLA_SCALE = (MLA_NOPE + MLA_ROPE) ** -0.5
MLA_V_PAD = 256
LOG2_E = 1.4426950408889634
ROPE_BASE = 10000.0

FFN_HIDDEN = 2816
N_BRANCH = 3

BIG_QKV = 0
BIG_ZA = BIG_QKV + GDN_QKV_W
BIG_ZB = BIG_ZA + GDN_V_W
BIG_XBC = BIG_ZB + SSD_INNER
BIG_CQKV = BIG_XBC + SSD_XBC
BIG_W = BIG_CQKV + MLA_Q_LORA + MLA_KV_LORA
SMALL_W = 256
LANE = 128
SUBLANE = 8
GDN_AB_LANE = 0
SSD_DT_LANE = 16

VMEM_LIMIT = 56 * 1024 * 1024

TM_FFN = 512
TM_MIX = 512
TM_MERGE = 256
TR_SEQ = 256
TQ_ATTN = 256
TK_ATTN = 512


def _cparams(*sem):
    return pltpu.CompilerParams(dimension_semantics=sem, vmem_limit_bytes=VMEM_LIMIT)


def _const_spec(shape):
    nd = len(shape)
    return pl.BlockSpec(shape, lambda *_: (0,) * nd, pipeline_mode=pl.Buffered(1))


def _rms(x, gain):
    return x * lax.rsqrt(jnp.mean(x * x, axis=-1, keepdims=True) + EPS) * gain


def _silu(x):
    return x * jax.nn.sigmoid(x)


def _softplus(x):
    return jnp.maximum(x, 0.0) + jnp.log1p(jnp.exp(-jnp.abs(x)))


def _dot(a, b):
    return jnp.dot(a, b, preferred_element_type=F32)


def _dot_nt(a, b):
    return lax.dot_general(a, b, (((1,), (1,)), ((), ())), preferred_element_type=F32)


def _dot_tn(a, b):
    return lax.dot_general(a, b, (((0,), (0,)), ((), ())), preferred_element_type=F32)


def _ffn_kernel(x_ref, g_ref, win_ref, wout_ref, fin_ref, o_ref, *, final):
    x = x_ref[...]
    xn = _rms(x, g_ref[...]).astype(BF16)
    gu = _dot(xn, win_ref[...])
    act = (_silu(gu[:, :FFN_HIDDEN]) * gu[:, FFN_HIDDEN:]).astype(BF16)
    h = x + 0.5 * _dot(act, wout_ref[...])
    if final:
        h = _rms(h, fin_ref[...])
    o_ref[...] = h


def _ffn(x2d, gain, w_in, w_out, fin, final):
    T = x2d.shape[0]
    tm = TM_FFN
    row = pl.BlockSpec((tm, D_MODEL), lambda i: (i, 0))
    return pl.pallas_call(
        functools.partial(_ffn_kernel, final=final),
        grid=(T // tm,),
        in_specs=[row, _const_spec((1, D_MODEL)), _const_spec((D_MODEL, 2 * FFN_HIDDEN)),
                  _const_spec((FFN_HIDDEN, D_MODEL)), _const_spec((1, D_MODEL))],
        out_specs=row,
        out_shape=jax.ShapeDtypeStruct((T, D_MODEL), F32),
        compiler_params=_cparams("parallel"),
        name="ffn",
    )(x2d, gain, w_in, w_out, fin)


def _mix_kernel(h_ref, g_ref, wbig_ref, wsmall_ref, qkv_ref, za_ref, zb_ref, xbc_ref, cqkv_ref, small_ref):
    u = _rms(h_ref[...], g_ref[...]).astype(BF16)
    big = _dot(u, wbig_ref[...])
    qkv_ref[...] = big[:, BIG_QKV:BIG_ZA].astype(BF16)
    za_ref[...] = big[:, BIG_ZA:BIG_ZB].astype(BF16)
    zb_ref[...] = big[:, BIG_ZB:BIG_XBC].astype(BF16)
    xbc_ref[...] = big[:, BIG_XBC:BIG_CQKV].astype(BF16)
    cqkv_ref[...] = big[:, BIG_CQKV:BIG_W].astype(BF16)
    small_ref[...] = _dot(u, wsmall_ref[...])


def _mix(h2d, gain, w_big, w_small):
    T = h2d.shape[0]
    tm = TM_MIX
    row = lambda w: pl.BlockSpec((tm, w), lambda i: (i, 0))
    widths = (GDN_QKV_W, GDN_V_W, SSD_INNER, SSD_XBC, MLA_Q_LORA + MLA_KV_LORA)
    return pl.pallas_call(
        _mix_kernel,
        grid=(T // tm,),
        in_specs=[row(D_MODEL), _const_spec((1, D_MODEL)), _const_spec((D_MODEL, BIG_W)),
                  _const_spec((D_MODEL, SMALL_W))],
        out_specs=[row(w) for w in widths] + [row(SMALL_W)],
        out_shape=[jax.ShapeDtypeStruct((T, w), BF16) for w in widths]
        + [jax.ShapeDtypeStruct((T, SMALL_W), F32)],
        compiler_params=_cparams("parallel"),
        name="mix",
    )(h2d, gain, w_big, w_small)


def _seq_specs(tr, width, L):
    per = tr // SUBLANE
    last = L // SUBLANE - 1
    cur = pl.BlockSpec((1, tr, width), lambda b, i: (b, i, 0))
    prev = pl.BlockSpec((1, SUBLANE, width), lambda b, i: (b, jnp.maximum(i * per - 1, 0), 0))
    nxt = pl.BlockSpec((1, SUBLANE, width), lambda b, i: (b, jnp.minimum((i + 1) * per, last), 0))
    return cur, prev, nxt


def _conv_tile(cur_ref, prev_ref, next_ref, w_ref):
    i = pl.program_id(1)
    tr = cur_ref.shape[1]
    x = cur_ref[0].astype(F32)
    prev = jnp.where(i > 0, prev_ref[0].astype(F32), 0.0)
    nxt = jnp.where(i < pl.num_programs(1) - 1, next_ref[0].astype(F32), 0.0)
    xe = jnp.concatenate([prev, x, nxt], axis=0)
    n = tr + 2 * SUBLANE
    acc = None
    for k in range(CONV_W):
        shifted = xe if k == CONV_HALF else pltpu.roll(xe, (CONV_HALF - k) % n, 0)
        term = shifted[SUBLANE:SUBLANE + tr] * w_ref[k:k + 1, :]
        acc = term if acc is None else acc + term
    return acc


def _chunk_cumsum(g, lane_is_fwd):
    tr = g.shape[0]
    pos = lax.broadcasted_iota(jnp.int32, g.shape, 0) & (CHUNK - 1)
    pre = g
    suf = g
    s = 1
    while s < CHUNK:
        pre = pre + jnp.where(pos >= s, pltpu.roll(pre, s, 0), 0.0)
        suf = suf + jnp.where(pos < CHUNK - s, pltpu.roll(suf, tr - s, 0), 0.0)
        s *= 2
    return jnp.where(lane_is_fwd, pre, suf)


def _gdn_prep_kernel(cur_ref, prev_ref, next_ref, small_ref, w_ref, alog_ref, bias_ref,
                     q_ref, k_ref, v_ref, gb_ref):
    qkv = _silu(_conv_tile(cur_ref, prev_ref, next_ref, w_ref))
    for h in range(GDN_HEADS):
        sl = slice(h * GDN_DK, (h + 1) * GDN_DK)
        q = qkv[:, sl]
        q_ref[0, :, sl] = (q * lax.rsqrt(jnp.sum(q * q, axis=-1, keepdims=True) + EPS)
                           * GDN_DK ** -0.5).astype(BF16)
        k = qkv[:, GDN_QK_W + h * GDN_DK:GDN_QK_W + (h + 1) * GDN_DK]
        k_ref[0, :, sl] = (k * lax.rsqrt(jnp.sum(k * k, axis=-1, keepdims=True) + EPS)).astype(BF16)
    v_ref[0] = qkv[:, 2 * GDN_QK_W:].astype(BF16)

    s = small_ref[0]
    lane = lax.broadcasted_iota(jnp.int32, s.shape, 1)
    g = jnp.where(lane < GDN_CHAINS, -jnp.exp(alog_ref[...]) * _softplus(s + bias_ref[...]), 0.0)
    gc = _chunk_cumsum(g, lane < GDN_HEADS)
    gb_ref[0] = jnp.where(lane < GDN_CHAINS, gc, jax.nn.sigmoid(s))


def _gdn_prep(qkv, small, conv_w, alog_row, bias_row):
    B, L, _ = qkv.shape
    tr = TR_SEQ
    cur, prev, nxt = _seq_specs(tr, GDN_QKV_W, L)
    out = lambda w: pl.BlockSpec((1, tr, w), lambda b, i: (b, i, 0))
    return pl.pallas_call(
        _gdn_prep_kernel,
        grid=(B, L // tr),
        in_specs=[cur, prev, nxt, pl.BlockSpec((1, tr, LANE), lambda b, i: (b, i, 0)),
                  _const_spec((SUBLANE, GDN_QKV_W)), _const_spec((1, LANE)), _const_spec((1, LANE))],
        out_specs=[out(GDN_QK_W), out(GDN_QK_W), out(GDN_V_W), out(LANE)],
        out_shape=[jax.ShapeDtypeStruct((B, L, GDN_QK_W), BF16), jax.ShapeDtypeStruct((B, L, GDN_QK_W), BF16),
                   jax.ShapeDtypeStruct((B, L, GDN_V_W), BF16), jax.ShapeDtypeStruct((B, L, LANE), F32)],
        compiler_params=_cparams("parallel", "parallel"),
        name="gdn_prep",
    )(qkv, qkv, qkv, small, conv_w, alog_row, bias_row)


NEUMANN_STEPS = 6
assert 2 ** NEUMANN_STEPS == CHUNK


def _gdn_scan_kernel(qf_ref, kf_ref, vf_ref, gbf_ref, grf_ref, qb_ref, kb_ref, vb_ref, gbb_ref, grb_ref,
                     of_ref, ob_ref, s_ref, *, nc):
    @pl.when(pl.program_id(1) == 0)
    def _():
        s_ref[...] = jnp.zeros_like(s_ref)

    dirs = ((qf_ref, kf_ref, vf_ref, gbf_ref, grf_ref, of_ref),
            (qb_ref, kb_ref, vb_ref, gbb_ref, grb_ref, ob_ref))
    ii = lax.broadcasted_iota(jnp.int32, (CHUNK, CHUNK), 0)
    jj = lax.broadcasted_iota(jnp.int32, (CHUNK, CHUNK), 1)

    units = []
    for step_idx in range(nc):
        for d, (q_ref, k_ref, v_ref, gb_ref, gr_ref, _) in enumerate(dirs):
            cc = step_idx if d == 0 else nc - 1 - step_idx
            rows = slice(cc * CHUNK, (cc + 1) * CHUNK)
            gb = gb_ref[0, rows, :]
            grow_all = gr_ref[0, cc]
            for h in range(GDN_HEADS):
                j = d * GDN_HEADS + h
                sl = slice(h * GDN_DK, (h + 1) * GDN_DK)
                units.append(dict(d=d, j=j, rows=rows, sl=sl, q=q_ref[0, rows, sl], k=k_ref[0, rows, sl],
                                  v=v_ref[0, rows, sl], gcol=gb[:, j:j + 1], grow=grow_all[j:j + 1, :],
                                  bcol=gb[:, GDN_CHAINS + j:GDN_CHAINS + j + 1]))
    for t in units:
        t["kk"] = _dot_nt(t["k"], t["k"])
        t["qk"] = _dot_nt(t["q"], t["k"])
    for t in units:
        gcol, bcol = t["gcol"], t["bcol"]
        incl = (ii >= jj) if t["d"] == 0 else (ii <= jj)
        dec = jnp.where(incl, jnp.exp(jnp.minimum(gcol - t["grow"], 0.0)), 0.0)
        t["n"] = -(bcol * t.pop("kk")) * jnp.where(ii == jj, 0.0, dec)
        t["attn"] = (t.pop("qk") * dec).astype(BF16)
        kf = t["k"].astype(F32)
        eg = jnp.exp(gcol)
        glast = gcol[CHUNK - 1:CHUNK] if t["d"] == 0 else gcol[0:1]
        t["x"] = bcol * jnp.concatenate([t["v"].astype(F32), kf * eg], axis=1)
        t["qd"] = t["q"].astype(F32) * eg
        t["kd"] = (kf * jnp.exp(glast - gcol)).astype(BF16)
        t["eglast"] = jnp.exp(glast)
    for step in range(NEUMANN_STEPS):
        for t in units:
            nb = t["n"].astype(BF16)
            t["y"] = _dot(nb, t["x"].astype(BF16))
            if step + 1 < NEUMANN_STEPS:
                t["n"] = _dot(nb, nb)
        for t in units:
            t["x"] = t["x"] + t.pop("y")
    for t in units:
        x = t.pop("x")
        t["u"] = x[:, :GDN_DV]
        t["wq"] = jnp.concatenate([x[:, GDN_DV:], t.pop("qd")], axis=0).astype(BF16)

    state = [s_ref[j] for j in range(GDN_CHAINS)]
    for step_idx in range(nc):
        group = units[step_idx * GDN_CHAINS:(step_idx + 1) * GDN_CHAINS]
        for t in group:
            t["r"] = _dot(t["wq"], state[t["j"]].astype(BF16))
        for t in group:
            t["vnew"] = (t["u"] - t["r"][:CHUNK]).astype(BF16)
        for t in group:
            t["o"] = t["r"][CHUNK:] + _dot(t["attn"], t["vnew"])
            state[t["j"]] = state[t["j"]] * t["eglast"] + _dot_tn(t["kd"], t["vnew"])
        for t in group:
            dirs[t["d"]][5][0, t["rows"], t["sl"]] = t["o"]
    for j in range(GDN_CHAINS):
        s_ref[j] = state[j]


def _gdn_scan(q, k, v, gb, grow):
    B, L, _ = q.shape
    tr = TR_SEQ
    nc = tr // CHUNK
    nt = L // tr
    fwd = lambda w: pl.BlockSpec((1, tr, w), lambda b, i: (b, i, 0))
    bwd = lambda w: pl.BlockSpec((1, tr, w), lambda b, i: (b, nt - 1 - i, 0))
    grf = pl.BlockSpec((1, nc, GDN_CHAINS, CHUNK), lambda b, i: (b, i, 0, 0))
    grb = pl.BlockSpec((1, nc, GDN_CHAINS, CHUNK), lambda b, i: (b, nt - 1 - i, 0, 0))
    return pl.pallas_call(
        functools.partial(_gdn_scan_kernel, nc=nc),
        grid=(B, nt),
        in_specs=[fwd(GDN_QK_W), fwd(GDN_QK_W), fwd(GDN_V_W), fwd(LANE), grf,
                  bwd(GDN_QK_W), bwd(GDN_QK_W), bwd(GDN_V_W), bwd(LANE), grb],
        out_specs=[fwd(GDN_V_W), bwd(GDN_V_W)],
        out_shape=[jax.ShapeDtypeStruct((B, L, GDN_V_W), F32)] * 2,
        scratch_shapes=[pltpu.VMEM((GDN_CHAINS, GDN_DK, GDN_DV), F32)],
        compiler_params=_cparams("arbitrary", "arbitrary"),
        name="gdn_scan",
    )(q, k, v, gb, grow, q, k, v, gb, grow)


def _ssd_prep_kernel(cur_ref, prev_ref, next_ref, small_ref, w_ref, cb_ref, alog_ref, bias_ref,
                     xbc_ref, dt_ref, ac_ref):
    xbc_ref[0] = _silu(_conv_tile(cur_ref, prev_ref, next_ref, w_ref) + cb_ref[...]).astype(BF16)
    s = small_ref[0]
    lane = lax.broadcasted_iota(jnp.int32, s.shape, 1)
    live = (lane >= SSD_DT_LANE) & (lane < SSD_DT_LANE + 2 * SSD_HEADS)
    dt = jnp.where(live, _softplus(s + bias_ref[...]), 0.0)
    dt_ref[0] = dt
    ac_ref[0] = _chunk_cumsum(dt * -jnp.exp(alog_ref[...]), lane < SSD_DT_LANE + SSD_HEADS)


def _ssd_prep(xbc, small, conv_w, conv_b, alog_row, bias_row):
    B, L, _ = xbc.shape
    tr = TR_SEQ
    cur, prev, nxt = _seq_specs(tr, SSD_XBC, L)
    out = lambda w: pl.BlockSpec((1, tr, w), lambda b, i: (b, i, 0))
    return pl.pallas_call(
        _ssd_prep_kernel,
        grid=(B, L // tr),
        in_specs=[cur, prev, nxt, pl.BlockSpec((1, tr, LANE), lambda b, i: (b, i, 0)),
                  _const_spec((SUBLANE, SSD_XBC)), _const_spec((1, SSD_XBC)),
                  _const_spec((1, LANE)), _const_spec((1, LANE))],
        out_specs=[out(SSD_XBC), out(LANE), out(LANE)],
        out_shape=[jax.ShapeDtypeStruct((B, L, SSD_XBC), BF16), jax.ShapeDtypeStruct((B, L, LANE), F32),
                   jax.ShapeDtypeStruct((B, L, LANE), F32)],
        compiler_params=_cparams("parallel", "parallel"),
        name="ssd_prep",
    )(xbc, xbc, xbc, small, conv_w, conv_b, alog_row, bias_row)


def _expand_heads(cols, base):
    rows = cols.shape[0]
    head = lax.broadcasted_iota(jnp.int32, (rows, SSD_INNER), 1) >> HEAD_SHIFT
    out = jnp.zeros((rows, SSD_INNER), F32)
    for h in range(SSD_HEADS):
        out = jnp.where(head == h, cols[:, base + h:base + h + 1], out)
    return out


def _ssd_scan_kernel(xf_ref, dtf_ref, acf_ref, arf_ref, xb_ref, dtb_ref, acb_ref, arb_ref,
                     yf_ref, yb_ref, h_ref, *, nc):
    @pl.when(pl.program_id(1) == 0)
    def _():
        h_ref[...] = jnp.zeros_like(h_ref)

    dirs = ((xf_ref, dtf_ref, acf_ref, arf_ref, yf_ref), (xb_ref, dtb_ref, acb_ref, arb_ref, yb_ref))
    ii = lax.broadcasted_iota(jnp.int32, (CHUNK, SSD_GROUP_W), 0)
    jj = lax.broadcasted_iota(jnp.int32, (CHUNK, SSD_GROUP_W), 1) & (CHUNK - 1)
    rb = lax.broadcasted_iota(jnp.int32, (SSD_GROUP_W, SSD_GROUP_W), 0) >> HEAD_SHIFT
    cb_ = lax.broadcasted_iota(jnp.int32, (SSD_GROUP_W, SSD_GROUP_W), 1) >> HEAD_SHIFT
    block_diag = rb == cb_

    units = []
    for step_idx in range(nc):
        for d, (x_ref, dt_ref, ac_ref, ar_ref, _) in enumerate(dirs):
            cc = step_idx if d == 0 else nc - 1 - step_idx
            rows = slice(cc * CHUNK, (cc + 1) * CHUNK)
            base = SSD_DT_LANE + d * SSD_HEADS
            acx = _expand_heads(ac_ref[0, rows, :], base)
            xr = x_ref[0, rows, :SSD_INNER].astype(F32) * _expand_heads(dt_ref[0, rows, :], base)
            aclast = acx[CHUNK - 1:CHUNK] if d == 0 else acx[0:1]
            acrow = ar_ref[0, cc]
            for g in range(SSD_GROUPS):
                sl = slice(g * SSD_GROUP_W, (g + 1) * SSD_GROUP_W)
                b0 = SSD_INNER + g * SSD_STATE
                c0 = SSD_INNER + (SSD_GROUPS + g) * SSD_STATE
                units.append(dict(d=d, g=g, rows=rows, sl=sl, bg=x_ref[0, rows, b0:b0 + SSD_STATE],
                                  cg=x_ref[0, rows, c0:c0 + SSD_STATE], acx=acx[:, sl], acrow=acrow[:, sl],
                                  aclast=aclast[:, sl], xr=xr[:, sl]))
    for t in units:
        t["cbt"] = _dot_nt(t["cg"], jnp.concatenate([t["bg"]] * SSD_HEADS_PER_GROUP, axis=0))
    for t in units:
        incl = (ii >= jj) if t["d"] == 0 else (ii <= jj)
        seg = jnp.where(incl, jnp.exp(jnp.minimum(t["acx"] - t["acrow"], 0.0)), 0.0)
        m = (t.pop("cbt") * seg).astype(BF16)
        xr = t.pop("xr")
        xr_bd = jnp.where(block_diag, jnp.concatenate([xr] * SSD_HEADS_PER_GROUP, axis=0), 0.0).astype(BF16)
        t["ydiag"] = _dot(m, xr_bd)
        t["st"] = _dot_tn(t["bg"], (xr * jnp.exp(t["aclast"] - t["acx"])).astype(BF16))

    state = {(d, g): h_ref[d, g] for d in range(2) for g in range(SSD_GROUPS)}
    per_step = 2 * SSD_GROUPS
    for step_idx in range(nc):
        for t in units[step_idx * per_step:(step_idx + 1) * per_step]:
            key = (t["d"], t["g"])
            y = t["ydiag"] + _dot(t["cg"], state[key].astype(BF16)) * jnp.exp(t["acx"])
            dirs[t["d"]][4][0, t["rows"], t["sl"]] = y
            state[key] = state[key] * jnp.exp(t["aclast"]) + t["st"]
    for (d, g), hval in state.items():
        h_ref[d, g] = hval


def _ssd_scan(xbc, dt, ac, acrow_f, acrow_b):
    B, L, _ = xbc.shape
    tr = TR_SEQ
    nc = tr // CHUNK
    nt = L // tr
    fwd = lambda w: pl.BlockSpec((1, tr, w), lambda b, i: (b, i, 0))
    bwd = lambda w: pl.BlockSpec((1, tr, w), lambda b, i: (b, nt - 1 - i, 0))
    arf = pl.BlockSpec((1, nc, 1, SSD_INNER), lambda b, i: (b, i, 0, 0))
    arb = pl.BlockSpec((1, nc, 1, SSD_INNER), lambda b, i: (b, nt - 1 - i, 0, 0))
    return pl.pallas_call(
        functools.partial(_ssd_scan_kernel, nc=nc),
        grid=(B, nt),
        in_specs=[fwd(SSD_XBC), fwd(LANE), fwd(LANE), arf, bwd(SSD_XBC), bwd(LANE), bwd(LANE), arb],
        out_specs=[fwd(SSD_INNER), bwd(SSD_INNER)],
        out_shape=[jax.ShapeDtypeStruct((B, L, SSD_INNER), F32)] * 2,
        scratch_shapes=[pltpu.VMEM((2, SSD_GROUPS, SSD_STATE, SSD_GROUP_W), F32)],
        compiler_params=_cparams("arbitrary", "arbitrary"),
        name="ssd_scan",
    )(xbc, dt, ac, acrow_f, xbc, dt, ac, acrow_b)


def _rope_pair(pair, cs):
    prod = pair * cs
    lane = lax.broadcasted_iota(jnp.int32, prod.shape, 1)
    return jnp.where(lane < MLA_ROPE, prod + pltpu.roll(prod, MLA_ROPE, 1), 0.0)


def _mla_prep_kernel(c_ref, small_ref, cs_ref, qn_ref, kvn_ref, wq_ref, wkv_ref, q_ref, k_ref, v_ref):
    c = c_ref[0].astype(F32)
    cs = cs_ref[...]
    cq = _rms(c[:, :MLA_Q_LORA], qn_ref[...]).astype(BF16)
    ckv = _rms(c[:, MLA_Q_LORA:], kvn_ref[...]).astype(BF16)
    qa = _dot(cq, wq_ref[...]) * (MLA_SCALE * LOG2_E)
    kva = _dot(ckv, wkv_ref[...])
    k_rope = _rope_pair(small_ref[0], cs).astype(BF16)
    for h in range(MLA_HEADS):
        base = h * MLA_QK_PAD
        q_ref[0, h, :, :MLA_NOPE] = qa[:, base:base + MLA_NOPE].astype(BF16)
        q_ref[0, h, :, MLA_NOPE:] = _rope_pair(qa[:, base + MLA_NOPE:base + MLA_QK_PAD], cs).astype(BF16)
        k_ref[0, h, :, :MLA_NOPE] = kva[:, base:base + MLA_NOPE].astype(BF16)
        k_ref[0, h, :, MLA_NOPE:] = k_rope
        v_ref[0, h, :, :MLA_V] = kva[:, base + MLA_NOPE:base + MLA_NOPE + MLA_V].astype(BF16)
        v_ref[0, h, :, MLA_V:] = jnp.ones((kva.shape[0], MLA_V_PAD - MLA_V), BF16)


def _mla_prep(cqkv, small, cs, q_norm, kv_norm, wq, wkv):
    B, L, _ = cqkv.shape
    tr = TR_SEQ
    head_out = lambda w: pl.BlockSpec((1, MLA_HEADS, tr, w), lambda b, i: (b, 0, i, 0))
    return pl.pallas_call(
        _mla_prep_kernel,
        grid=(B, L // tr),
        in_specs=[pl.BlockSpec((1, tr, MLA_Q_LORA + MLA_KV_LORA), lambda b, i: (b, i, 0)),
                  pl.BlockSpec((1, tr, LANE), lambda b, i: (b, i, 1)),
                  pl.BlockSpec((tr, LANE), lambda b, i: (i, 0)),
                  _const_spec((1, MLA_Q_LORA)), _const_spec((1, MLA_KV_LORA)),
                  _const_spec((MLA_Q_LORA, MLA_HEADS * MLA_QK_PAD)),
                  _const_spec((MLA_KV_LORA, MLA_HEADS * (MLA_NOPE + MLA_V)))],
        out_specs=[head_out(MLA_QK_PAD), head_out(MLA_QK_PAD), head_out(MLA_V_PAD)],
        out_shape=[jax.ShapeDtypeStruct((B, MLA_HEADS, L, MLA_QK_PAD), BF16),
                   jax.ShapeDtypeStruct((B, MLA_HEADS, L, MLA_QK_PAD), BF16),
                   jax.ShapeDtypeStruct((B, MLA_HEADS, L, MLA_V_PAD), BF16)],
        compiler_params=_cparams("parallel", "parallel"),
        name="mla_prep",
    )(cqkv, small, cs, q_norm, kv_norm, wq, wkv)


def _mla_attn_kernel(q_ref, k_ref, v_ref, o_ref):
    q = q_ref[0, 0]
    tq = q.shape[0]
    nk = k_ref.shape[2] // TK_ATTN
    m = jnp.full((tq, 1), -jnp.inf, F32)
    acc = jnp.zeros((tq, MLA_V_PAD), F32)
    s_next = _dot_nt(q, k_ref[0, 0, 0:TK_ATTN, :])
    for j in range(nk):
        s = s_next
        if j + 1 < nk:
            s_next = _dot_nt(q, k_ref[0, 0, (j + 1) * TK_ATTN:(j + 2) * TK_ATTN, :])
        m_new = jnp.maximum(m, jnp.max(s, axis=-1, keepdims=True))
        p = jnp.exp2(s - m_new).astype(BF16)
        acc = jnp.exp2(m - m_new) * acc + _dot(p, v_ref[0, 0, j * TK_ATTN:(j + 1) * TK_ATTN, :])
        m = m_new
    o_ref[0] = (acc[:, :MLA_V] / acc[:, MLA_V:]).astype(BF16)


def _mla_attn(q, k, v):
    B, H, L, _ = q.shape
    tq = TQ_ATTN
    return pl.pallas_call(
        _mla_attn_kernel,
        grid=(B, H, L // tq),
        in_specs=[pl.BlockSpec((1, 1, tq, MLA_QK_PAD), lambda b, h, i: (b, h, i, 0)),
                  pl.BlockSpec((1, 1, L, MLA_QK_PAD), lambda b, h, i: (b, h, 0, 0)),
                  pl.BlockSpec((1, 1, L, MLA_V_PAD), lambda b, h, i: (b, h, 0, 0))],
        out_specs=pl.BlockSpec((1, tq, MLA_V), lambda b, h, i: (b, i, h)),
        out_shape=jax.ShapeDtypeStruct((B, L, H * MLA_V), BF16),
        compiler_params=_cparams("parallel", "parallel", "parallel"),
        name="mla_attn",
    )(q, k, v)


def _merge_kernel(h_ref, of_ref, ob_ref, za_ref, yf_ref, yb_ref, xbc_ref, zb_ref, oc_ref,
                  mixg_ref, gng_ref, dskip_ref, sng_ref, wa_ref, wb_ref, wc_ref, wg_ref, bg_ref, wo_ref,
                  out_ref):
    h = h_ref[...]
    u = _rms(h, mixg_ref[...]).astype(BF16)
    gates = jax.nn.sigmoid(_dot(u, wg_ref[...]) + bg_ref[...])

    o = of_ref[...] + ob_ref[...]
    za = za_ref[...].astype(F32)
    parts = []
    for hd in range(GDN_HEADS):
        sl = slice(hd * GDN_DV, (hd + 1) * GDN_DV)
        parts.append(_rms(o[:, sl], gng_ref[...]) * _silu(za[:, sl]))
    ya = _dot(jnp.concatenate(parts, axis=1).astype(BF16), wa_ref[...])

    y = yf_ref[...] + yb_ref[...] + xbc_ref[...].astype(F32) * dskip_ref[...]
    y = _rms(y * _silu(zb_ref[...].astype(F32)), sng_ref[...])
    yb = _dot(y.astype(BF16), wb_ref[...])

    yc = _dot(oc_ref[...], wc_ref[...])

    merged = (gates[:, :D_MODEL] * ya + gates[:, D_MODEL:2 * D_MODEL] * yb
              + gates[:, 2 * D_MODEL:] * yc)
    out_ref[...] = h + _dot(merged.astype(BF16), wo_ref[...])


def _merge(h2d, o_f, o_b, za, y_f, y_b, xbc_act, zb, oc, mix_gain, gdn_gain, dskip, ssd_gain,
           wa, wb, wc, wg, bg, wo):
    T = h2d.shape[0]
    tm = TM_MERGE
    row = lambda w: pl.BlockSpec((tm, w), lambda i: (i, 0))
    return pl.pallas_call(
        _merge_kernel,
        grid=(T // tm,),
        in_specs=[row(D_MODEL), row(GDN_V_W), row(GDN_V_W), row(GDN_V_W), row(SSD_INNER), row(SSD_INNER),
                  row(SSD_INNER), row(SSD_INNER), row(MLA_HEADS * MLA_V),
                  _const_spec((1, D_MODEL)), _const_spec((1, GDN_DV)), _const_spec((1, SSD_INNER)),
                  _const_spec((1, SSD_INNER)), _const_spec((GDN_V_W, D_MODEL)),
                  _const_spec((SSD_INNER, D_MODEL)), _const_spec((MLA_HEADS * MLA_V, D_MODEL)),
                  _const_spec((D_MODEL, N_BRANCH * D_MODEL)), _const_spec((1, N_BRANCH * D_MODEL)),
                  _const_spec((D_MODEL, D_MODEL))],
        out_specs=row(D_MODEL),
        out_shape=jax.ShapeDtypeStruct((T, D_MODEL), F32),
        compiler_params=_cparams("parallel"),
        name="merge",
    )(h2d, o_f, o_b, za, y_f, y_b, xbc_act, zb, oc, mix_gain, gdn_gain, dskip, ssd_gain,
      wa, wb, wc, wg, bg, wo)


def _rotate_half_cols(w):
    w1, w2 = jnp.split(w, 2, axis=-1)
    return jnp.concatenate([-w2, w1], axis=-1)


def _lane_row(values, base):
    row = jnp.zeros((1, LANE), F32)
    return row.at[0, base:base + values.shape[0]].set(values.astype(F32))


def _prep_layer(p, i):
    w_in = p["w_in"][i]
    c = 0
    seg = {}
    for name, width in (("qkv", GDN_QKV_W), ("za", GDN_V_W), ("ab", 4 * GDN_HEADS), ("zb", SSD_INNER),
                        ("xbc", SSD_XBC), ("dt", 2 * SSD_HEADS), ("cq", MLA_Q_LORA), ("ckv", MLA_KV_LORA),
                        ("kr", MLA_ROPE)):
        seg[name] = w_in[:, c:c + width]
        c += width
    w_big = jnp.concatenate([seg["qkv"], seg["za"], seg["zb"], seg["xbc"], seg["cq"], seg["ckv"]], axis=1)
    pad = jnp.zeros((D_MODEL, LANE - 4 * GDN_HEADS - 2 * SSD_HEADS), F32)
    w_small = jnp.concatenate([seg["ab"], seg["dt"], pad, seg["kr"], _rotate_half_cols(seg["kr"])], axis=1)

    wq = p["mla_w_uq"][i].reshape(MLA_Q_LORA, MLA_HEADS, MLA_NOPE + MLA_ROPE)
    wq_rope = wq[..., MLA_NOPE:]
    wq = jnp.concatenate([wq, _rotate_half_cols(wq_rope)], axis=-1).reshape(MLA_Q_LORA, MLA_HEADS * MLA_QK_PAD)

    row = lambda v: v.reshape(1, -1).astype(F32)
    conv_pad = lambda w: jnp.concatenate([w, jnp.zeros((SUBLANE - CONV_W, w.shape[1]), F32)], axis=0)
    return dict(
        ffn1_norm=row(p["ffn1_norm"][i]), w_ffn1_in=p["w_ffn1_in"][i].astype(BF16),
        w_ffn1_out=p["w_ffn1_out"][i].astype(BF16),
        mix_norm=row(p["mix_norm"][i]), w_big=w_big.astype(BF16), w_small=w_small.astype(BF16),
        gdn_conv=conv_pad(p["gdn_conv"][i]),
        gdn_alog=_lane_row(p["gdn_A_log"][i].reshape(-1), GDN_AB_LANE),
        gdn_bias=_lane_row(p["gdn_dt_bias"][i].reshape(-1), GDN_AB_LANE),
        gdn_norm=row(p["gdn_norm"][i]),
        ssd_conv=conv_pad(p["ssd_conv"][i]), ssd_conv_b=row(p["ssd_conv_b"][i]),
        ssd_alog=_lane_row(p["ssd_A_log"][i].reshape(-1), SSD_DT_LANE),
        ssd_bias=_lane_row(p["ssd_dt_bias"][i].reshape(-1), SSD_DT_LANE),
        ssd_dskip=row(jnp.repeat(p["ssd_D"][i], SSD_HEAD_DIM)), ssd_norm=row(p["ssd_norm"][i]),
        mla_q_norm=row(p["mla_q_norm"][i]), mla_kv_norm=row(p["mla_kv_norm"][i]),
        mla_wq=wq.astype(BF16), mla_wkv=p["mla_w_ukv"][i].astype(BF16),
        w_branch_a=p["w_branch_a"][i].astype(BF16), w_branch_b=p["w_branch_b"][i].astype(BF16),
        w_branch_c=p["w_branch_c"][i].astype(BF16), w_gate=p["w_gate"][i].astype(BF16),
        b_gate=row(p["b_gate"][i]), w_out=p["w_out"][i].astype(BF16),
        ffn2_norm=row(p["ffn2_norm"][i]), w_ffn2_in=p["w_ffn2_in"][i].astype(BF16),
        w_ffn2_out=p["w_ffn2_out"][i].astype(BF16),
    )


def _rope_table(L):
    inv_freq = jnp.power(ROPE_BASE, -jnp.arange(0, MLA_ROPE, 2, dtype=F32) / MLA_ROPE)
    ang = jnp.arange(L, dtype=F32)[:, None] * inv_freq[None, :]
    ang = jnp.concatenate([ang, ang], axis=-1)
    return jnp.concatenate([jnp.cos(ang), jnp.sin(ang)], axis=-1)


def _rows_to_chunk_rows(cols, B, L):
    n = cols.shape[-1]
    return cols.reshape(B, L // CHUNK, CHUNK, n).transpose(0, 1, 3, 2)


def _layer(x2d, B, L, cs, lp, final_gain, final):
    T = B * L
    h = _ffn(x2d, lp["ffn1_norm"], lp["w_ffn1_in"], lp["w_ffn1_out"], final_gain, False)
    qkv, za, zb, xbc, cqkv, small = _mix(h, lp["mix_norm"], lp["w_big"], lp["w_small"])
    seq = lambda a: a.reshape(B, L, a.shape[-1])
    small3 = seq(small)

    q, k, v, gb = _gdn_prep(seq(qkv), small3, lp["gdn_conv"], lp["gdn_alog"], lp["gdn_bias"])
    grow = _rows_to_chunk_rows(gb[..., :GDN_CHAINS], B, L)
    o_f, o_b = _gdn_scan(q, k, v, gb, grow)

    xbc_act, dt, ac = _ssd_prep(seq(xbc), small3, lp["ssd_conv"], lp["ssd_conv_b"], lp["ssd_alog"],
                                lp["ssd_bias"])
    acrow = _rows_to_chunk_rows(ac[..., SSD_DT_LANE:SSD_DT_LANE + 2 * SSD_HEADS], B, L)
    acrow = acrow.reshape(B, L // CHUNK, 2, 1, SSD_HEADS, 1, CHUNK)
    acrow = jnp.broadcast_to(acrow, (B, L // CHUNK, 2, 1, SSD_HEADS, SSD_HEAD_DIM // CHUNK, CHUNK))
    acrow = acrow.reshape(B, L // CHUNK, 2, 1, SSD_INNER)
    y_f, y_b = _ssd_scan(xbc_act, dt, ac, acrow[:, :, 0], acrow[:, :, 1])

    qc, kc, vc = _mla_prep(seq(cqkv), small3, cs, lp["mla_q_norm"], lp["mla_kv_norm"], lp["mla_wq"],
                           lp["mla_wkv"])
    oc = _mla_attn(qc, kc, vc)

    flat = lambda a: a.reshape(T, a.shape[-1])
    h = _merge(h, flat(o_f), flat(o_b), za, flat(y_f), flat(y_b), flat(xbc_act), zb, flat(oc),
               lp["mix_norm"], lp["gdn_norm"], lp["ssd_dskip"], lp["ssd_norm"],
               lp["w_branch_a"], lp["w_branch_b"], lp["w_branch_c"], lp["w_gate"], lp["b_gate"], lp["w_out"])
    return _ffn(h, lp["ffn2_norm"], lp["w_ffn2_in"], lp["w_ffn2_out"], final_gain, final)


def _trunk(x, layers, final_gain):
    B, L, _ = x.shape
    cs = _rope_table(L)
    h = x.reshape(B * L, D_MODEL)
    for i, lp in enumerate(layers):
        h = _layer(h, B, L, cs, lp, final_gain, i == len(layers) - 1)
    return h.reshape(B, L, D_MODEL)


def kernel(x_prompt, x_sample, ffn1_norm, w_ffn1_in, w_ffn1_out, mix_norm, w_in, gdn_conv, gdn_A_log, gdn_dt_bias, gdn_norm, ssd_conv, ssd_conv_b, ssd_A_log, ssd_dt_bias, ssd_D, ssd_norm, mla_q_norm, mla_w_uq, mla_kv_norm, mla_w_ukv, w_branch_a, w_branch_b, w_branch_c, w_gate, b_gate, w_out, ffn2_norm, w_ffn2_in, w_ffn2_out, final_norm):
    p = dict(ffn1_norm=ffn1_norm, w_ffn1_in=w_ffn1_in, w_ffn1_out=w_ffn1_out, mix_norm=mix_norm, w_in=w_in,
             gdn_conv=gdn_conv, gdn_A_log=gdn_A_log, gdn_dt_bias=gdn_dt_bias, gdn_norm=gdn_norm,
             ssd_conv=ssd_conv, ssd_conv_b=ssd_conv_b, ssd_A_log=ssd_A_log, ssd_dt_bias=ssd_dt_bias,
             ssd_D=ssd_D, ssd_norm=ssd_norm, mla_q_norm=mla_q_norm, mla_w_uq=mla_w_uq,
             mla_kv_norm=mla_kv_norm, mla_w_ukv=mla_w_ukv, w_branch_a=w_branch_a, w_branch_b=w_branch_b,
             w_branch_c=w_branch_c, w_gate=w_gate, b_gate=b_gate, w_out=w_out, ffn2_norm=ffn2_norm,
             w_ffn2_in=w_ffn2_in, w_ffn2_out=w_ffn2_out)
    layers = [_prep_layer(p, i) for i in range(DEPTH)]
    final_gain = final_norm.reshape(1, D_MODEL).astype(F32)
    return (_trunk(x_prompt, layers, final_gain), _trunk(x_sample, layers, final_gain))
```

```python
import functools

import jax
import jax.numpy as jnp
from jax import lax
from jax.experimental import pallas as pl
from jax.experimental.pallas import tpu as pltpu

F32 = jnp.float32
BF16 = jnp.bfloat16

D_MODEL = 1024
DEPTH = 2
EPS = 1e-6
CONV_W = 5
CONV_HALF = CONV_W // 2
CHUNK = 64

GDN_HEADS = 4
GDN_DK = 128
GDN_DV = 128
GDN_QK_W = GDN_HEADS * GDN_DK
GDN_V_W = GDN_HEADS * GDN_DV
GDN_QKV_W = 2 * GDN_QK_W + GDN_V_W
GDN_CHAINS = 2 * GDN_HEADS

SSD_HEADS = 8
SSD_HEAD_DIM = 64
SSD_INNER = SSD_HEADS * SSD_HEAD_DIM
SSD_GROUPS = 2
SSD_STATE = 128
SSD_XBC = SSD_INNER + 2 * SSD_GROUPS * SSD_STATE
SSD_GROUP_W = SSD_INNER // SSD_GROUPS
SSD_HEADS_PER_GROUP = SSD_HEADS // SSD_GROUPS
HEAD_SHIFT = 6
assert (1 << HEAD_SHIFT) == SSD_HEAD_DIM == CHUNK

MLA_HEADS = 4
MLA_Q_LORA = 256
MLA_KV_LORA = 256
MLA_NOPE = 128
MLA_ROPE = 64
MLA_V = 128
MLA_QK_PAD = 256
MLA_SCALE = (MLA_NOPE + MLA_ROPE) ** -0.5
MLA_V_PAD = 256
LOG2_E = 1.4426950408889634
ROPE_BASE = 10000.0

FFN_HIDDEN = 2816
N_BRANCH = 3

BIG_QKV = 0
BIG_ZA = BIG_QKV + GDN_QKV_W
BIG_ZB = BIG_ZA + GDN_V_W
BIG_XBC = BIG_ZB + SSD_INNER
BIG_CQKV = BIG_XBC + SSD_XBC
BIG_W = BIG_CQKV + MLA_Q_LORA + MLA_KV_LORA
SMALL_W = 256
LANE = 128
SUBLANE = 8
GDN_AB_LANE = 0
SSD_DT_LANE = 16

VMEM_LIMIT = 56 * 1024 * 1024

TM_FFN = 512
TM_MIX = 512
TM_MERGE = 512
TR_SEQ = 512
TQ_ATTN = 512
TK_ATTN = 512
ATTN_KV_VMEM_BUDGET = 32 * 1024 * 1024


def _cparams(*sem):
    return pltpu.CompilerParams(dimension_semantics=sem, vmem_limit_bytes=VMEM_LIMIT)


def _const_spec(shape):
    nd = len(shape)
    return pl.BlockSpec(shape, lambda *_: (0,) * nd, pipeline_mode=pl.Buffered(1))


def _rms(x, gain):
    return x * lax.rsqrt(jnp.mean(x * x, axis=-1, keepdims=True) + EPS) * gain


def _silu(x):
    return x * jax.nn.sigmoid(x)


def _softplus(x):
    return jnp.maximum(x, 0.0) + jnp.log1p(jnp.exp(-jnp.abs(x)))


def _dot(a, b):
    return jnp.dot(a, b, preferred_element_type=F32)


def _dot_nt(a, b):
    return lax.dot_general(a, b, (((1,), (1,)), ((), ())), preferred_element_type=F32)


def _dot_tn(a, b):
    return lax.dot_general(a, b, (((0,), (0,)), ((), ())), preferred_element_type=F32)


def _ffn_kernel(x_ref, g_ref, win_ref, wout_ref, fin_ref, o_ref, *, final):
    x = x_ref[...]
    xn = _rms(x, g_ref[...]).astype(BF16)
    gu = _dot(xn, win_ref[...])
    act = (_silu(gu[:, :FFN_HIDDEN]) * gu[:, FFN_HIDDEN:]).astype(BF16)
    h = x + 0.5 * _dot(act, wout_ref[...])
    if final:
        h = _rms(h, fin_ref[...])
    o_ref[...] = h


def _ffn(x2d, gain, w_in, w_out, fin, final):
    T = x2d.shape[0]
    tm = TM_FFN
    row = pl.BlockSpec((tm, D_MODEL), lambda i: (i, 0))
    return pl.pallas_call(
        functools.partial(_ffn_kernel, final=final),
        grid=(T // tm,),
        in_specs=[row, _const_spec((1, D_MODEL)), _const_spec((D_MODEL, 2 * FFN_HIDDEN)),
                  _const_spec((FFN_HIDDEN, D_MODEL)), _const_spec((1, D_MODEL))],
        out_specs=row,
        out_shape=jax.ShapeDtypeStruct((T, D_MODEL), F32),
        compiler_params=_cparams("parallel"),
        name="ffn",
    )(x2d, gain, w_in, w_out, fin)


def _mix_kernel(h_ref, g_ref, wbig_ref, wsmall_ref, qkv_ref, za_ref, zb_ref, xbc_ref, cqkv_ref, small_ref):
    u = _rms(h_ref[...], g_ref[...]).astype(BF16)
    big = _dot(u, wbig_ref[...])
    qkv_ref[...] = big[:, BIG_QKV:BIG_ZA].astype(BF16)
    za_ref[...] = big[:, BIG_ZA:BIG_ZB].astype(BF16)
    zb_ref[...] = big[:, BIG_ZB:BIG_XBC].astype(BF16)
    xbc_ref[...] = big[:, BIG_XBC:BIG_CQKV].astype(BF16)
    cqkv_ref[...] = big[:, BIG_CQKV:BIG_W].astype(BF16)
    small_ref[...] = _dot(u, wsmall_ref[...])


def _mix(h2d, gain, w_big, w_small):
    T = h2d.shape[0]
    tm = TM_MIX
    row = lambda w: pl.BlockSpec((tm, w), lambda i: (i, 0))
    widths = (GDN_QKV_W, GDN_V_W, SSD_INNER, SSD_XBC, MLA_Q_LORA + MLA_KV_LORA)
    return pl.pallas_call(
        _mix_kernel,
        grid=(T // tm,),
        in_specs=[row(D_MODEL), _const_spec((1, D_MODEL)), _const_spec((D_MODEL, BIG_W)),
                  _const_spec((D_MODEL, SMALL_W))],
        out_specs=[row(w) for w in widths] + [row(SMALL_W)],
        out_shape=[jax.ShapeDtypeStruct((T, w), BF16) for w in widths]
        + [jax.ShapeDtypeStruct((T, SMALL_W), F32)],
        compiler_params=_cparams("parallel"),
        name="mix",
    )(h2d, gain, w_big, w_small)


def _seq_specs(tr, width, L):
    per = tr // SUBLANE
    last = L // SUBLANE - 1
    cur = pl.BlockSpec((1, tr, width), lambda b, i: (b, i, 0))
    prev = pl.BlockSpec((1, SUBLANE, width), lambda b, i: (b, jnp.maximum(i * per - 1, 0), 0))
    nxt = pl.BlockSpec((1, SUBLANE, width), lambda b, i: (b, jnp.minimum((i + 1) * per, last), 0))
    return cur, prev, nxt


def _conv_tile(cur_ref, prev_ref, next_ref, w_ref):
    i = pl.program_id(1)
    tr = cur_ref.shape[1]
    x = cur_ref[0].astype(F32)
    prev = jnp.where(i > 0, prev_ref[0].astype(F32), 0.0)
    nxt = jnp.where(i < pl.num_programs(1) - 1, next_ref[0].astype(F32), 0.0)
    xe = jnp.concatenate([prev, x, nxt], axis=0)
    n = tr + 2 * SUBLANE
    acc = None
    for k in range(CONV_W):
        shifted = xe if k == CONV_HALF else pltpu.roll(xe, (CONV_HALF - k) % n, 0)
        term = shifted[SUBLANE:SUBLANE + tr] * w_ref[k:k + 1, :]
        acc = term if acc is None else acc + term
    return acc


def _chunk_cumsum(g, lane_is_fwd):
    tr = g.shape[0]
    pos = lax.broadcasted_iota(jnp.int32, g.shape, 0) & (CHUNK - 1)
    pre = g
    suf = g
    s = 1
    while s < CHUNK:
        pre = pre + jnp.where(pos >= s, pltpu.roll(pre, s, 0), 0.0)
        suf = suf + jnp.where(pos < CHUNK - s, pltpu.roll(suf, tr - s, 0), 0.0)
        s *= 2
    return jnp.where(lane_is_fwd, pre, suf)


def _gdn_prep_kernel(cur_ref, prev_ref, next_ref, small_ref, w_ref, alog_ref, bias_ref,
                     q_ref, k_ref, v_ref, gb_ref):
    qkv = _silu(_conv_tile(cur_ref, prev_ref, next_ref, w_ref))
    for h in range(GDN_HEADS):
        sl = slice(h * GDN_DK, (h + 1) * GDN_DK)
        q = qkv[:, sl]
        q_ref[0, :, sl] = (q * lax.rsqrt(jnp.sum(q * q, axis=-1, keepdims=True) + EPS)
                           * GDN_DK ** -0.5).astype(BF16)
        k = qkv[:, GDN_QK_W + h * GDN_DK:GDN_QK_W + (h + 1) * GDN_DK]
        k_ref[0, :, sl] = (k * lax.rsqrt(jnp.sum(k * k, axis=-1, keepdims=True) + EPS)).astype(BF16)
    v_ref[0] = qkv[:, 2 * GDN_QK_W:].astype(BF16)

    s = small_ref[0]
    lane = lax.broadcasted_iota(jnp.int32, s.shape, 1)
    g = jnp.where(lane < GDN_CHAINS, -jnp.exp(alog_ref[...]) * _softplus(s + bias_ref[...]), 0.0)
    gc = _chunk_cumsum(g, lane < GDN_HEADS)
    gb_ref[0] = jnp.where(lane < GDN_CHAINS, gc, jax.nn.sigmoid(s))


def _gdn_prep(qkv, small, conv_w, alog_row, bias_row):
    B, L, _ = qkv.shape
    tr = TR_SEQ
    cur, prev, nxt = _seq_specs(tr, GDN_QKV_W, L)
    out = lambda w: pl.BlockSpec((1, tr, w), lambda b, i: (b, i, 0))
    return pl.pallas_call(
        _gdn_prep_kernel,
        grid=(B, L // tr),
        in_specs=[cur, prev, nxt, pl.BlockSpec((1, tr, LANE), lambda b, i: (b, i, 0)),
                  _const_spec((SUBLANE, GDN_QKV_W)), _const_spec((1, LANE)), _const_spec((1, LANE))],
        out_specs=[out(GDN_QK_W), out(GDN_QK_W), out(GDN_V_W), out(LANE)],
        out_shape=[jax.ShapeDtypeStruct((B, L, GDN_QK_W), BF16), jax.ShapeDtypeStruct((B, L, GDN_QK_W), BF16),
                   jax.ShapeDtypeStruct((B, L, GDN_V_W), BF16), jax.ShapeDtypeStruct((B, L, LANE), F32)],
        compiler_params=_cparams("parallel", "parallel"),
        name="gdn_prep",
    )(qkv, qkv, qkv, small, conv_w, alog_row, bias_row)


NEUMANN_STEPS = 6
assert 2 ** NEUMANN_STEPS == CHUNK


def _gdn_scan_kernel(qf_ref, kf_ref, vf_ref, gbf_ref, grf_ref, qb_ref, kb_ref, vb_ref, gbb_ref, grb_ref,
                     of_ref, ob_ref, s_ref, *, nc):
    @pl.when(pl.program_id(1) == 0)
    def _():
        s_ref[...] = jnp.zeros_like(s_ref)

    dirs = ((qf_ref, kf_ref, vf_ref, gbf_ref, grf_ref, of_ref),
            (qb_ref, kb_ref, vb_ref, gbb_ref, grb_ref, ob_ref))
    ii = lax.broadcasted_iota(jnp.int32, (CHUNK, CHUNK), 0)
    jj = lax.broadcasted_iota(jnp.int32, (CHUNK, CHUNK), 1)

    units = []
    for step_idx in range(nc):
        for d, (q_ref, k_ref, v_ref, gb_ref, gr_ref, _) in enumerate(dirs):
            cc = step_idx if d == 0 else nc - 1 - step_idx
            rows = slice(cc * CHUNK, (cc + 1) * CHUNK)
            gb = gb_ref[0, rows, :]
            grow_all = gr_ref[0, cc]
            for h in range(GDN_HEADS):
                j = d * GDN_HEADS + h
                sl = slice(h * GDN_DK, (h + 1) * GDN_DK)
                units.append(dict(d=d, j=j, rows=rows, sl=sl, q=q_ref[0, rows, sl], k=k_ref[0, rows, sl],
                                  v=v_ref[0, rows, sl], gcol=gb[:, j:j + 1], grow=grow_all[j:j + 1, :],
                                  bcol=gb[:, GDN_CHAINS + j:GDN_CHAINS + j + 1]))
    for t in units:
        r = _dot_nt(jnp.concatenate([t["k"], t["q"]], axis=0), t["k"])
        t["kk"] = r[:CHUNK]
        t["qk"] = r[CHUNK:]
    for t in units:
        gcol, bcol = t["gcol"], t["bcol"]
        incl = (ii >= jj) if t["d"] == 0 else (ii <= jj)
        dec = jnp.where(incl, jnp.exp(jnp.minimum(gcol - t["grow"], 0.0)), 0.0)
        t["n"] = -(bcol * t.pop("kk")) * jnp.where(ii == jj, 0.0, dec)
        t["attn"] = (t.pop("qk") * dec).astype(BF16)
        kf = t["k"].astype(F32)
        eg = jnp.exp(gcol)
        glast = gcol[CHUNK - 1:CHUNK] if t["d"] == 0 else gcol[0:1]
        t["x"] = bcol * jnp.concatenate([t["v"].astype(F32), kf * eg], axis=1)
        t["qd"] = t["q"].astype(F32) * eg
        t["kd"] = (kf * jnp.exp(glast - gcol)).astype(BF16)
        t["eglast"] = jnp.exp(glast)
    for step in range(NEUMANN_STEPS):
        for t in units:
            nb = t["n"].astype(BF16)
            t["y"] = _dot(nb, t["x"].astype(BF16))
            if step + 1 < NEUMANN_STEPS:
                t["n"] = _dot(nb, nb)
        for t in units:
            t["x"] = t["x"] + t.pop("y")
    for t in units:
        x = t.pop("x")
        t["u"] = x[:, :GDN_DV]
        t["wq"] = jnp.concatenate([x[:, GDN_DV:], t.pop("qd")], axis=0).astype(BF16)

    state = [s_ref[j] for j in range(GDN_CHAINS)]
    for step_idx in range(nc):
        group = units[step_idx * GDN_CHAINS:(step_idx + 1) * GDN_CHAINS]
        for t in group:
            t["r"] = _dot(t["wq"], state[t["j"]].astype(BF16))
        for t in group:
            t["vnew"] = (t["u"] - t["r"][:CHUNK]).astype(BF16)
        for t in group:
            t["o"] = t["r"][CHUNK:] + _dot(t["attn"], t["vnew"])
            state[t["j"]] = state[t["j"]] * t["eglast"] + _dot_tn(t["kd"], t["vnew"])
        for t in group:
            dirs[t["d"]][5][0, t["rows"], t["sl"]] = t["o"]
    for j in range(GDN_CHAINS):
        s_ref[j] = state[j]


def _gdn_scan(q, k, v, gb, grow):
    B, L, _ = q.shape
    tr = TR_SEQ
    nc = tr // CHUNK
    nt = L // tr
    fwd = lambda w: pl.BlockSpec((1, tr, w), lambda b, i: (b, i, 0))
    bwd = lambda w: pl.BlockSpec((1, tr, w), lambda b, i: (b, nt - 1 - i, 0))
    grf = pl.BlockSpec((1, nc, GDN_CHAINS, CHUNK), lambda b, i: (b, i, 0, 0))
    grb = pl.BlockSpec((1, nc, GDN_CHAINS, CHUNK), lambda b, i: (b, nt - 1 - i, 0, 0))
    return pl.pallas_call(
        functools.partial(_gdn_scan_kernel, nc=nc),
        grid=(B, nt),
        in_specs=[fwd(GDN_QK_W), fwd(GDN_QK_W), fwd(GDN_V_W), fwd(LANE), grf,
                  bwd(GDN_QK_W), bwd(GDN_QK_W), bwd(GDN_V_W), bwd(LANE), grb],
        out_specs=[fwd(GDN_V_W), bwd(GDN_V_W)],
        out_shape=[jax.ShapeDtypeStruct((B, L, GDN_V_W), F32)] * 2,
        scratch_shapes=[pltpu.VMEM((GDN_CHAINS, GDN_DK, GDN_DV), F32)],
        compiler_params=_cparams("arbitrary", "arbitrary"),
        name="gdn_scan",
    )(q, k, v, gb, grow, q, k, v, gb, grow)


def _ssd_prep_kernel(cur_ref, prev_ref, next_ref, small_ref, w_ref, cb_ref, alog_ref, bias_ref,
                     xbc_ref, dt_ref, ac_ref):
    xbc_ref[0] = _silu(_conv_tile(cur_ref, prev_ref, next_ref, w_ref) + cb_ref[...]).astype(BF16)
    s = small_ref[0]
    lane = lax.broadcasted_iota(jnp.int32, s.shape, 1)
    live = (lane >= SSD_DT_LANE) & (lane < SSD_DT_LANE + 2 * SSD_HEADS)
    dt = jnp.where(live, _softplus(s + bias_ref[...]), 0.0)
    dt_ref[0] = dt
    ac_ref[0] = _chunk_cumsum(dt * -jnp.exp(alog_ref[...]), lane < SSD_DT_LANE + SSD_HEADS)


def _ssd_prep(xbc, small, conv_w, conv_b, alog_row, bias_row):
    B, L, _ = xbc.shape
    tr = TR_SEQ
    cur, prev, nxt = _seq_specs(tr, SSD_XBC, L)
    out = lambda w: pl.BlockSpec((1, tr, w), lambda b, i: (b, i, 0))
    return pl.pallas_call(
        _ssd_prep_kernel,
        grid=(B, L // tr),
        in_specs=[cur, prev, nxt, pl.BlockSpec((1, tr, LANE), lambda b, i: (b, i, 0)),
                  _const_spec((SUBLANE, SSD_XBC)), _const_spec((1, SSD_XBC)),
                  _const_spec((1, LANE)), _const_spec((1, LANE))],
        out_specs=[out(SSD_XBC), out(LANE), out(LANE)],
        out_shape=[jax.ShapeDtypeStruct((B, L, SSD_XBC), BF16), jax.ShapeDtypeStruct((B, L, LANE), F32),
                   jax.ShapeDtypeStruct((B, L, LANE), F32)],
        compiler_params=_cparams("parallel", "parallel"),
        name="ssd_prep",
    )(xbc, xbc, xbc, small, conv_w, conv_b, alog_row, bias_row)


def _expand_heads(cols, base):
    rows = cols.shape[0]
    low_half = lax.broadcasted_iota(jnp.int32, (rows, LANE), 1) < SSD_HEAD_DIM
    tiles = []
    for h in range(0, SSD_HEADS, LANE // SSD_HEAD_DIM):
        tiles.append(jnp.where(low_half, cols[:, base + h:base + h + 1], cols[:, base + h + 1:base + h + 2]))
    return jnp.concatenate(tiles, axis=1)


def _ssd_scan_kernel(xf_ref, dtf_ref, acf_ref, arf_ref, xb_ref, dtb_ref, acb_ref, arb_ref,
                     yf_ref, yb_ref, h_ref, *, nc):
    @pl.when(pl.program_id(1) == 0)
    def _():
        h_ref[...] = jnp.zeros_like(h_ref)

    dirs = ((xf_ref, dtf_ref, acf_ref, arf_ref, yf_ref), (xb_ref, dtb_ref, acb_ref, arb_ref, yb_ref))
    ii = lax.broadcasted_iota(jnp.int32, (CHUNK, SSD_GROUP_W), 0)
    jj = lax.broadcasted_iota(jnp.int32, (CHUNK, SSD_GROUP_W), 1) & (CHUNK - 1)
    rb = lax.broadcasted_iota(jnp.int32, (SSD_GROUP_W, SSD_GROUP_W), 0) >> HEAD_SHIFT
    cb_ = lax.broadcasted_iota(jnp.int32, (SSD_GROUP_W, SSD_GROUP_W), 1) >> HEAD_SHIFT
    block_diag = rb == cb_

    units = []
    for step_idx in range(nc):
        for d, (x_ref, dt_ref, ac_ref, ar_ref, _) in enumerate(dirs):
            cc = step_idx if d == 0 else nc - 1 - step_idx
            rows = slice(cc * CHUNK, (cc + 1) * CHUNK)
            base = SSD_DT_LANE + d * SSD_HEADS
            acx = _expand_heads(ac_ref[0, rows, :], base)
            xr = x_ref[0, rows, :SSD_INNER].astype(F32) * _expand_heads(dt_ref[0, rows, :], base)
            aclast = acx[CHUNK - 1:CHUNK] if d == 0 else acx[0:1]
            acrow = ar_ref[0, cc]
            for g in range(SSD_GROUPS):
                sl = slice(g * SSD_GROUP_W, (g + 1) * SSD_GROUP_W)
                b0 = SSD_INNER + g * SSD_STATE
                c0 = SSD_INNER + (SSD_GROUPS + g) * SSD_STATE
                units.append(dict(d=d, g=g, rows=rows, sl=sl, bg=x_ref[0, rows, b0:b0 + SSD_STATE],
                                  cg=x_ref[0, rows, c0:c0 + SSD_STATE], acx=acx[:, sl], acrow=acrow[:, sl],
                                  aclast=aclast[:, sl], xr=xr[:, sl]))
    for t in units:
        t["cbt"] = _dot_nt(t["cg"], jnp.concatenate([t["bg"]] * SSD_HEADS_PER_GROUP, axis=0))
    for t in units:
        incl = (ii >= jj) if t["d"] == 0 else (ii <= jj)
        seg = jnp.where(incl, jnp.exp(jnp.minimum(t["acx"] - t["acrow"], 0.0)), 0.0)
        m = (t.pop("cbt") * seg).astype(BF16)
        xr = t.pop("xr")
        xr_bd = jnp.where(block_diag, jnp.concatenate([xr] * SSD_HEADS_PER_GROUP, axis=0), 0.0).astype(BF16)
        t["ydiag"] = _dot(m, xr_bd)
        t["st"] = _dot_tn(t["bg"], (xr * jnp.exp(t["aclast"] - t["acx"])).astype(BF16))

    state = {(d, g): h_ref[d, g] for d in range(2) for g in range(SSD_GROUPS)}
    per_step = 2 * SSD_GROUPS
    for step_idx in range(nc):
        for t in units[step_idx * per_step:(step_idx + 1) * per_step]:
            key = (t["d"], t["g"])
            y = t["ydiag"] + _dot(t["cg"], state[key].astype(BF16)) * jnp.exp(t["acx"])
            dirs[t["d"]][4][0, t["rows"], t["sl"]] = y
            state[key] = state[key] * jnp.exp(t["aclast"]) + t["st"]
    for (d, g), hval in state.items():
        h_ref[d, g] = hval


def _ssd_scan(xbc, dt, ac, acrow_f, acrow_b):
    B, L, _ = xbc.shape
    tr = TR_SEQ
    nc = tr // CHUNK
    nt = L // tr
    fwd = lambda w: pl.BlockSpec((1, tr, w), lambda b, i: (b, i, 0))
    bwd = lambda w: pl.BlockSpec((1, tr, w), lambda b, i: (b, nt - 1 - i, 0))
    arf = pl.BlockSpec((1, nc, 1, SSD_INNER), lambda b, i: (b, i, 0, 0))
    arb = pl.BlockSpec((1, nc, 1, SSD_INNER), lambda b, i: (b, nt - 1 - i, 0, 0))
    return pl.pallas_call(
        functools.partial(_ssd_scan_kernel, nc=nc),
        grid=(B, nt),
        in_specs=[fwd(SSD_XBC), fwd(LANE), fwd(LANE), arf, bwd(SSD_XBC), bwd(LANE), bwd(LANE), arb],
        out_specs=[fwd(SSD_INNER), bwd(SSD_INNER)],
        out_shape=[jax.ShapeDtypeStruct((B, L, SSD_INNER), F32)] * 2,
        scratch_shapes=[pltpu.VMEM((2, SSD_GROUPS, SSD_STATE, SSD_GROUP_W), F32)],
        compiler_params=_cparams("arbitrary", "arbitrary"),
        name="ssd_scan",
    )(xbc, dt, ac, acrow_f, xbc, dt, ac, acrow_b)


def _rope_pair(pair, cs):
    prod = pair * cs
    lane = lax.broadcasted_iota(jnp.int32, prod.shape, 1)
    return jnp.where(lane < MLA_ROPE, prod + pltpu.roll(prod, MLA_ROPE, 1), 0.0)


def _mla_prep_kernel(c_ref, small_ref, cs_ref, qn_ref, kvn_ref, wq_ref, wkv_ref, q_ref, k_ref, v_ref):
    c = c_ref[0].astype(F32)
    cs = cs_ref[...]
    cq = _rms(c[:, :MLA_Q_LORA], qn_ref[...]).astype(BF16)
    ckv = _rms(c[:, MLA_Q_LORA:], kvn_ref[...]).astype(BF16)
    qa = _dot(cq, wq_ref[...]) * (MLA_SCALE * LOG2_E)
    kva = _dot(ckv, wkv_ref[...])
    k_rope = _rope_pair(small_ref[0], cs).astype(BF16)
    for h in range(MLA_HEADS):
        base = h * MLA_QK_PAD
        q_ref[0, h, :, :MLA_NOPE] = qa[:, base:base + MLA_NOPE].astype(BF16)
        q_ref[0, h, :, MLA_NOPE:] = _rope_pair(qa[:, base + MLA_NOPE:base + MLA_QK_PAD], cs).astype(BF16)
        k_ref[0, h, :, :MLA_NOPE] = kva[:, base:base + MLA_NOPE].astype(BF16)
        k_ref[0, h, :, MLA_NOPE:] = k_rope
        v_ref[0, h, :, :MLA_V] = kva[:, base + MLA_NOPE:base + MLA_NOPE + MLA_V].astype(BF16)
        v_ref[0, h, :, MLA_V:] = jnp.ones((kva.shape[0], MLA_V_PAD - MLA_V), BF16)


def _mla_prep(cqkv, small, cs, q_norm, kv_norm, wq, wkv):
    B, L, _ = cqkv.shape
    tr = TR_SEQ
    head_out = lambda w: pl.BlockSpec((1, MLA_HEADS, tr, w), lambda b, i: (b, 0, i, 0))
    return pl.pallas_call(
        _mla_prep_kernel,
        grid=(B, L // tr),
        in_specs=[pl.BlockSpec((1, tr, MLA_Q_LORA + MLA_KV_LORA), lambda b, i: (b, i, 0)),
                  pl.BlockSpec((1, tr, LANE), lambda b, i: (b, i, 1)),
                  pl.BlockSpec((tr, LANE), lambda b, i: (i, 0)),
                  _const_spec((1, MLA_Q_LORA)), _const_spec((1, MLA_KV_LORA)),
                  _const_spec((MLA_Q_LORA, MLA_HEADS * MLA_QK_PAD)),
                  _const_spec((MLA_KV_LORA, MLA_HEADS * (MLA_NOPE + MLA_V)))],
        out_specs=[head_out(MLA_QK_PAD), head_out(MLA_QK_PAD), head_out(MLA_V_PAD)],
        out_shape=[jax.ShapeDtypeStruct((B, MLA_HEADS, L, MLA_QK_PAD), BF16),
                   jax.ShapeDtypeStruct((B, MLA_HEADS, L, MLA_QK_PAD), BF16),
                   jax.ShapeDtypeStruct((B, MLA_HEADS, L, MLA_V_PAD), BF16)],
        compiler_params=_cparams("parallel", "parallel"),
        name="mla_prep",
    )(cqkv, small, cs, q_norm, kv_norm, wq, wkv)


def _mla_attn_kernel(q_ref, k_ref, v_ref, o_ref):
    hs, tq = q_ref.shape[1], q_ref.shape[2]
    nk = k_ref.shape[2] // TK_ATTN
    steps = [(h, j) for h in range(hs) for j in range(nk)]

    def scores(h, j):
        return _dot_nt(q_ref[0, h], k_ref[0, h, j * TK_ATTN:(j + 1) * TK_ATTN, :])

    s_next = scores(*steps[0])
    for idx, (h, j) in enumerate(steps):
        s = s_next
        if idx + 1 < len(steps):
            s_next = scores(*steps[idx + 1])
        if j == 0:
            m = jnp.full((tq, 1), -jnp.inf, F32)
            acc = jnp.zeros((tq, MLA_V_PAD), F32)
        m_new = jnp.maximum(m, jnp.max(s, axis=-1, keepdims=True))
        p = jnp.exp2(s - m_new).astype(BF16)
        acc = jnp.exp2(m - m_new) * acc + _dot(p, v_ref[0, h, j * TK_ATTN:(j + 1) * TK_ATTN, :])
        m = m_new
        if j == nk - 1:
            o_ref[0, :, h * MLA_V:(h + 1) * MLA_V] = (acc[:, :MLA_V] / acc[:, MLA_V:]).astype(BF16)


def _mla_attn(q, k, v):
    B, H, L, _ = q.shape
    tq = TQ_ATTN
    kv_bytes_per_head = 2 * L * (MLA_QK_PAD + MLA_V_PAD) * 2
    hs = max(1, min(H, ATTN_KV_VMEM_BUDGET // kv_bytes_per_head))
    while H % hs:
        hs -= 1
    return pl.pallas_call(
        _mla_attn_kernel,
        grid=(B, H // hs, L // tq),
        in_specs=[pl.BlockSpec((1, hs, tq, MLA_QK_PAD), lambda b, h, i: (b, h, i, 0)),
                  pl.BlockSpec((1, hs, L, MLA_QK_PAD), lambda b, h, i: (b, h, 0, 0)),
                  pl.BlockSpec((1, hs, L, MLA_V_PAD), lambda b, h, i: (b, h, 0, 0))],
        out_specs=pl.BlockSpec((1, tq, hs * MLA_V), lambda b, h, i: (b, i, h)),
        out_shape=jax.ShapeDtypeStruct((B, L, H * MLA_V), BF16),
        compiler_params=_cparams("parallel", "parallel", "parallel"),
        name="mla_attn",
    )(q, k, v)


def _merge_kernel(h_ref, of_ref, ob_ref, za_ref, yf_ref, yb_ref, xbc_ref, zb_ref, oc_ref,
                  mixg_ref, gng_ref, dskip_ref, sng_ref, wa_ref, wb_ref, wc_ref, wg_ref, bg_ref, wo_ref,
                  out_ref):
    h = h_ref[...]
    u = _rms(h, mixg_ref[...]).astype(BF16)
    gates = jax.nn.sigmoid(_dot(u, wg_ref[...]) + bg_ref[...])

    o = of_ref[...] + ob_ref[...]
    za = za_ref[...].astype(F32)
    parts = []
    for hd in range(GDN_HEADS):
        sl = slice(hd * GDN_DV, (hd + 1) * GDN_DV)
        parts.append(_rms(o[:, sl], gng_ref[...]) * _silu(za[:, sl]))
    ya = _dot(jnp.concatenate(parts, axis=1).astype(BF16), wa_ref[...])

    y = yf_ref[...] + yb_ref[...] + xbc_ref[...].astype(F32) * dskip_ref[...]
    y = _rms(y * _silu(zb_ref[...].astype(F32)), sng_ref[...])
    yb = _dot(y.astype(BF16), wb_ref[...])

    yc = _dot(oc_ref[...], wc_ref[...])

    merged = (gates[:, :D_MODEL] * ya + gates[:, D_MODEL:2 * D_MODEL] * yb
              + gates[:, 2 * D_MODEL:] * yc)
    out_ref[...] = h + _dot(merged.astype(BF16), wo_ref[...])


def _merge(h2d, o_f, o_b, za, y_f, y_b, xbc_act, zb, oc, mix_gain, gdn_gain, dskip, ssd_gain,
           wa, wb, wc, wg, bg, wo):
    T = h2d.shape[0]
    tm = TM_MERGE
    row = lambda w: pl.BlockSpec((tm, w), lambda i: (i, 0))
    return pl.pallas_call(
        _merge_kernel,
        grid=(T // tm,),
        in_specs=[row(D_MODEL), row(GDN_V_W), row(GDN_V_W), row(GDN_V_W), row(SSD_INNER), row(SSD_INNER),
                  row(SSD_INNER), row(SSD_INNER), row(MLA_HEADS * MLA_V),
                  _const_spec((1, D_MODEL)), _const_spec((1, GDN_DV)), _const_spec((1, SSD_INNER)),
                  _const_spec((1, SSD_INNER)), _const_spec((GDN_V_W, D_MODEL)),
                  _const_spec((SSD_INNER, D_MODEL)), _const_spec((MLA_HEADS * MLA_V, D_MODEL)),
                  _const_spec((D_MODEL, N_BRANCH * D_MODEL)), _const_spec((1, N_BRANCH * D_MODEL)),
                  _const_spec((D_MODEL, D_MODEL))],
        out_specs=row(D_MODEL),
        out_shape=jax.ShapeDtypeStruct((T, D_MODEL), F32),
        compiler_params=_cparams("parallel"),
        name="merge",
    )(h2d, o_f, o_b, za, y_f, y_b, xbc_act, zb, oc, mix_gain, gdn_gain, dskip, ssd_gain,
      wa, wb, wc, wg, bg, wo)


def _rotate_half_cols(w):
    w1, w2 = jnp.split(w, 2, axis=-1)
    return jnp.concatenate([-w2, w1], axis=-1)


def _lane_row(values, base):
    row = jnp.zeros((1, LANE), F32)
    return row.at[0, base:base + values.shape[0]].set(values.astype(F32))


def _prep_layer(p, i):
    w_in = p["w_in"][i]
    c = 0
    seg = {}
    for name, width in (("qkv", GDN_QKV_W), ("za", GDN_V_W), ("ab", 4 * GDN_HEADS), ("zb", SSD_INNER),
                        ("xbc", SSD_XBC), ("dt", 2 * SSD_HEADS), ("cq", MLA_Q_LORA), ("ckv", MLA_KV_LORA),
                        ("kr", MLA_ROPE)):
        seg[name] = w_in[:, c:c + width]
        c += width
    w_big = jnp.concatenate([seg["qkv"], seg["za"], seg["zb"], seg["xbc"], seg["cq"], seg["ckv"]], axis=1)
    pad = jnp.zeros((D_MODEL, LANE - 4 * GDN_HEADS - 2 * SSD_HEADS), F32)
    w_small = jnp.concatenate([seg["ab"], seg["dt"], pad, seg["kr"], _rotate_half_cols(seg["kr"])], axis=1)

    wq = p["mla_w_uq"][i].reshape(MLA_Q_LORA, MLA_HEADS, MLA_NOPE + MLA_ROPE)
    wq_rope = wq[..., MLA_NOPE:]
    wq = jnp.concatenate([wq, _rotate_half_cols(wq_rope)], axis=-1).reshape(MLA_Q_LORA, MLA_HEADS * MLA_QK_PAD)

    row = lambda v: v.reshape(1, -1).astype(F32)
    conv_pad = lambda w: jnp.concatenate([w, jnp.zeros((SUBLANE - CONV_W, w.shape[1]), F32)], axis=0)
    return dict(
        ffn1_norm=row(p["ffn1_norm"][i]), w_ffn1_in=p["w_ffn1_in"][i].astype(BF16),
        w_ffn1_out=p["w_ffn1_out"][i].astype(BF16),
        mix_norm=row(p["mix_norm"][i]), w_big=w_big.astype(BF16), w_small=w_small.astype(BF16),
        gdn_conv=conv_pad(p["gdn_conv"][i]),
        gdn_alog=_lane_row(p["gdn_A_log"][i].reshape(-1), GDN_AB_LANE),
        gdn_bias=_lane_row(p["gdn_dt_bias"][i].reshape(-1), GDN_AB_LANE),
        gdn_norm=row(p["gdn_norm"][i]),
        ssd_conv=conv_pad(p["ssd_conv"][i]), ssd_conv_b=row(p["ssd_conv_b"][i]),
        ssd_alog=_lane_row(p["ssd_A_log"][i].reshape(-1), SSD_DT_LANE),
        ssd_bias=_lane_row(p["ssd_dt_bias"][i].reshape(-1), SSD_DT_LANE),
        ssd_dskip=row(jnp.repeat(p["ssd_D"][i], SSD_HEAD_DIM)), ssd_norm=row(p["ssd_norm"][i]),
        mla_q_norm=row(p["mla_q_norm"][i]), mla_kv_norm=row(p["mla_kv_norm"][i]),
        mla_wq=wq.astype(BF16), mla_wkv=p["mla_w_ukv"][i].astype(BF16),
        w_branch_a=p["w_branch_a"][i].astype(BF16), w_branch_b=p["w_branch_b"][i].astype(BF16),
        w_branch_c=p["w_branch_c"][i].astype(BF16), w_gate=p["w_gate"][i].astype(BF16),
        b_gate=row(p["b_gate"][i]), w_out=p["w_out"][i].astype(BF16),
        ffn2_norm=row(p["ffn2_norm"][i]), w_ffn2_in=p["w_ffn2_in"][i].astype(BF16),
        w_ffn2_out=p["w_ffn2_out"][i].astype(BF16),
    )


def _rope_table(L):
    inv_freq = jnp.power(ROPE_BASE, -jnp.arange(0, MLA_ROPE, 2, dtype=F32) / MLA_ROPE)
    ang = jnp.arange(L, dtype=F32)[:, None] * inv_freq[None, :]
    ang = jnp.concatenate([ang, ang], axis=-1)
    return jnp.concatenate([jnp.cos(ang), jnp.sin(ang)], axis=-1)


def _rows_to_chunk_rows(cols, B, L):
    n = cols.shape[-1]
    return cols.reshape(B, L // CHUNK, CHUNK, n).transpose(0, 1, 3, 2)


def _layer(x2d, B, L, cs, lp, final_gain, final):
    T = B * L
    h = _ffn(x2d, lp["ffn1_norm"], lp["w_ffn1_in"], lp["w_ffn1_out"], final_gain, False)
    qkv, za, zb, xbc, cqkv, small = _mix(h, lp["mix_norm"], lp["w_big"], lp["w_small"])
    seq = lambda a: a.reshape(B, L, a.shape[-1])
    small3 = seq(small)

    q, k, v, gb = _gdn_prep(seq(qkv), small3, lp["gdn_conv"], lp["gdn_alog"], lp["gdn_bias"])
    grow = _rows_to_chunk_rows(gb[..., :GDN_CHAINS], B, L)
    o_f, o_b = _gdn_scan(q, k, v, gb, grow)

    xbc_act, dt, ac = _ssd_prep(seq(xbc), small3, lp["ssd_conv"], lp["ssd_conv_b"], lp["ssd_alog"],
                                lp["ssd_bias"])
    acrow = _rows_to_chunk_rows(ac[..., SSD_DT_LANE:SSD_DT_LANE + 2 * SSD_HEADS], B, L)
    acrow = acrow.reshape(B, L // CHUNK, 2, 1, SSD_HEADS, 1, CHUNK)
    acrow = jnp.broadcast_to(acrow, (B, L // CHUNK, 2, 1, SSD_HEADS, SSD_HEAD_DIM // CHUNK, CHUNK))
    acrow = acrow.reshape(B, L // CHUNK, 2, 1, SSD_INNER)
    y_f, y_b = _ssd_scan(xbc_act, dt, ac, acrow[:, :, 0], acrow[:, :, 1])

    qc, kc, vc = _mla_prep(seq(cqkv), small3, cs, lp["mla_q_norm"], lp["mla_kv_norm"], lp["mla_wq"],
                           lp["mla_wkv"])
    oc = _mla_attn(qc, kc, vc)

    flat = lambda a: a.reshape(T, a.shape[-1])
    h = _merge(h, flat(o_f), flat(o_b), za, flat(y_f), flat(y_b), flat(xbc_act), zb, flat(oc),
               lp["mix_norm"], lp["gdn_norm"], lp["ssd_dskip"], lp["ssd_norm"],
               lp["w_branch_a"], lp["w_branch_b"], lp["w_branch_c"], lp["w_gate"], lp["b_gate"], lp["w_out"])
    return _ffn(h, lp["ffn2_norm"], lp["w_ffn2_in"], lp["w_ffn2_out"], final_gain, final)


def _trunk(x, layers, final_gain):
    B, L, _ = x.shape
    cs = _rope_table(L)
    h = x.reshape(B * L, D_MODEL)
    for i, lp in enumerate(layers):
        h = _layer(h, B, L, cs, lp, final_gain, i == len(layers) - 1)
    return h.reshape(B, L, D_MODEL)


def kernel(x_prompt, x_sample, ffn1_norm, w_ffn1_in, w_ffn1_out, mix_norm, w_in, gdn_conv, gdn_A_log, gdn_dt_bias, gdn_norm, ssd_conv, ssd_conv_b, ssd_A_log, ssd_dt_bias, ssd_D, ssd_norm, mla_q_norm, mla_w_uq, mla_kv_norm, mla_w_ukv, w_branch_a, w_branch_b, w_branch_c, w_gate, b_gate, w_out, ffn2_norm, w_ffn2_in, w_ffn2_out, final_norm):
    p = dict(ffn1_norm=ffn1_norm, w_ffn1_in=w_ffn1_in, w_ffn1_out=w_ffn1_out, mix_norm=mix_norm, w_in=w_in,
             gdn_conv=gdn_conv, gdn_A_log=gdn_A_log, gdn_dt_bias=gdn_dt_bias, gdn_norm=gdn_norm,
             ssd_conv=ssd_conv, ssd_conv_b=ssd_conv_b, ssd_A_log=ssd_A_log, ssd_dt_bias=ssd_dt_bias,
             ssd_D=ssd_D, ssd_norm=ssd_norm, mla_q_norm=mla_q_norm, mla_w_uq=mla_w_uq,
             mla_kv_norm=mla_kv_norm, mla_w_ukv=mla_w_ukv, w_branch_a=w_branch_a, w_branch_b=w_branch_b,
             w_branch_c=w_branch_c, w_gate=w_gate, b_gate=b_gate, w_out=w_out, ffn2_norm=ffn2_norm,
             w_ffn2_in=w_ffn2_in, w_ffn2_out=w_ffn2_out)
    layers = [_prep_layer(p, i) for i in range(DEPTH)]
    final_gain = final_norm.reshape(1, D_MODEL).astype(F32)
    return (_trunk(x_prompt, layers, final_gain), _trunk(x_sample, layers, final_gain))
```

```python
import functools

import jax
import jax.numpy as jnp
from jax import lax
from jax.experimental import pallas as pl
from jax.experimental.pallas import tpu as pltpu

F32 = jnp.float32
BF16 = jnp.bfloat16

D_MODEL = 1024
DEPTH = 2
EPS = 1e-6
CONV_W = 5
CONV_HALF = CONV_W // 2
CHUNK = 64

GDN_HEADS = 4
GDN_DK = 128
GDN_DV = 128
GDN_QK_W = GDN_HEADS * GDN_DK
GDN_V_W = GDN_HEADS * GDN_DV
GDN_QKV_W = 2 * GDN_QK_W + GDN_V_W
GDN_CHAINS = 2 * GDN_HEADS

SSD_HEADS = 8
SSD_HEAD_DIM = 64
SSD_INNER = SSD_HEADS * SSD_HEAD_DIM
SSD_GROUPS = 2
SSD_STATE = 128
SSD_XBC = SSD_INNER + 2 * SSD_GROUPS * SSD_STATE
SSD_GROUP_W = SSD_INNER // SSD_GROUPS
SSD_HEADS_PER_GROUP = SSD_HEADS // SSD_GROUPS
HEAD_SHIFT = 6
assert (1 << HEAD_SHIFT) == SSD_HEAD_DIM == CHUNK

MLA_HEADS = 4
MLA_Q_LORA = 256
MLA_KV_LORA = 256
MLA_NOPE = 128
MLA_ROPE = 64
MLA_V = 128
MLA_QK_PAD = 256
MLA_SCALE = (MLA_NOPE + MLA_ROPE) ** -0.5
MLA_V_PAD = 256
LOG2_E = 1.4426950408889634
ROPE_BASE = 10000.0

FFN_HIDDEN = 2816
N_BRANCH = 3

CONV_GROUP_W = 512
SMALL_W = 256
LANE = 128
SUBLANE = 8
GDN_AB_LANE = 0
SSD_DT_LANE = 16

VMEM_LIMIT = 56 * 1024 * 1024

TM_FFN = 512
TM_MIX = 512
TM_MERGE = 512
TR_SEQ = 512
TQ_ATTN = 512
TK_ATTN = 512
ATTN_KV_VMEM_BUDGET = 32 * 1024 * 1024


def _cparams(*sem):
    return pltpu.CompilerParams(dimension_semantics=sem, vmem_limit_bytes=VMEM_LIMIT)


def _const_spec(shape):
    nd = len(shape)
    return pl.BlockSpec(shape, lambda *_: (0,) * nd, pipeline_mode=pl.Buffered(1))


def _rms(x, gain):
    return x * lax.rsqrt(jnp.mean(x * x, axis=-1, keepdims=True) + EPS) * gain


def _silu(x):
    return x * jax.nn.sigmoid(x)


def _softplus(x):
    return jnp.maximum(x, 0.0) + jnp.log1p(jnp.exp(-jnp.abs(x)))


def _dot(a, b):
    return jnp.dot(a, b, preferred_element_type=F32)


def _dot_nt(a, b):
    return lax.dot_general(a, b, (((1,), (1,)), ((), ())), preferred_element_type=F32)


def _dot_tn(a, b):
    return lax.dot_general(a, b, (((0,), (0,)), ((), ())), preferred_element_type=F32)


def _ffn_kernel(x_ref, g_ref, win_ref, wout_ref, fin_ref, o_ref, *, final):
    x = x_ref[...]
    xn = _rms(x, g_ref[...]).astype(BF16)
    gu = _dot(xn, win_ref[...])
    act = (_silu(gu[:, :FFN_HIDDEN]) * gu[:, FFN_HIDDEN:]).astype(BF16)
    h = x + 0.5 * _dot(act, wout_ref[...])
    if final:
        h = _rms(h, fin_ref[...])
    o_ref[...] = h


def _ffn(x2d, gain, w_in, w_out, fin, final):
    T = x2d.shape[0]
    tm = TM_FFN
    row = pl.BlockSpec((tm, D_MODEL), lambda i: (i, 0))
    return pl.pallas_call(
        functools.partial(_ffn_kernel, final=final),
        grid=(T // tm,),
        in_specs=[row, _const_spec((1, D_MODEL)), _const_spec((D_MODEL, 2 * FFN_HIDDEN)),
                  _const_spec((FFN_HIDDEN, D_MODEL)), _const_spec((1, D_MODEL))],
        out_specs=row,
        out_shape=jax.ShapeDtypeStruct((T, D_MODEL), F32),
        compiler_params=_cparams("parallel"),
        name="ffn",
    )(x2d, gain, w_in, w_out, fin)


def _mix_kernel(cur_ref, prev_ref, next_ref, cs_ref, g_ref, wconv_ref, wrest_ref, wsmall_ref,
                gconv_ref, galog_ref, gbias_ref, sconv_ref, sconvb_ref, salog_ref, sbias_ref,
                qn_ref, kvn_ref, wq_ref, wkv_ref,
                q_ref, k_ref, v_ref, gb_ref, za_ref, zb_ref, xbc_ref, dt_ref, ac_ref,
                mq_ref, mk_ref, mv_ref):
    i = pl.program_id(1)
    tm = cur_ref.shape[1]
    body = slice(SUBLANE, SUBLANE + tm)
    first = i == 0
    last = i == pl.num_programs(1) - 1
    hx = jnp.concatenate([prev_ref[0], cur_ref[0], next_ref[0]], axis=0)
    u = _rms(hx, g_ref[...]).astype(BF16)

    def conv_cols(c0, w_ref, wc0):
        cols = _dot(u, wconv_ref[:, c0:c0 + CONV_GROUP_W])
        xe = jnp.concatenate([jnp.where(first, 0.0, cols[:SUBLANE]), cols[body],
                              jnp.where(last, 0.0, cols[SUBLANE + tm:])], axis=0)
        return _conv(xe, w_ref, wc0, CONV_GROUP_W)

    for c0, o_ref, scale in ((0, q_ref, GDN_DK ** -0.5), (GDN_QK_W, k_ref, 1.0)):
        a = _silu(conv_cols(c0, gconv_ref, c0))
        for h in range(GDN_HEADS):
            sl = slice(h * GDN_DK, (h + 1) * GDN_DK)
            seg = a[:, sl]
            o_ref[0, :, sl] = (seg * (lax.rsqrt(jnp.sum(seg * seg, axis=-1, keepdims=True) + EPS)
                                      * scale)).astype(BF16)
    v_ref[0] = _silu(conv_cols(2 * GDN_QK_W, gconv_ref, 2 * GDN_QK_W)).astype(BF16)

    for c0 in range(0, SSD_XBC, CONV_GROUP_W):
        xbc_ref[0, :, c0:c0 + CONV_GROUP_W] = _silu(
            conv_cols(GDN_QKV_W + c0, sconv_ref, c0) + sconvb_ref[:, c0:c0 + CONV_GROUP_W]).astype(BF16)

    rest = _dot(u, wrest_ref[...])[body]
    za_ref[0] = rest[:, :GDN_V_W].astype(BF16)
    zb_ref[0] = rest[:, GDN_V_W:GDN_V_W + SSD_INNER].astype(BF16)
    small = _dot(u, wsmall_ref[...])[body]

    s = small[:, :LANE]
    lane = lax.broadcasted_iota(jnp.int32, s.shape, 1)
    g = jnp.where(lane < GDN_CHAINS, -jnp.exp(galog_ref[...]) * _softplus(s + gbias_ref[...]), 0.0)
    gb_ref[0] = jnp.where(lane < GDN_CHAINS, _chunk_cumsum(g, lane < GDN_HEADS), jax.nn.sigmoid(s))
    live = (lane >= SSD_DT_LANE) & (lane < SSD_DT_LANE + 2 * SSD_HEADS)
    dt = jnp.where(live, _softplus(s + sbias_ref[...]), 0.0)
    dt_ref[0] = dt
    ac_ref[0] = _chunk_cumsum(dt * -jnp.exp(salog_ref[...]), lane < SSD_DT_LANE + SSD_HEADS)

    c = rest[:, GDN_V_W + SSD_INNER:]
    cs = cs_ref[...]
    cq = _rms(c[:, :MLA_Q_LORA], qn_ref[...]).astype(BF16)
    ckv = _rms(c[:, MLA_Q_LORA:], kvn_ref[...]).astype(BF16)
    qa = _dot(cq, wq_ref[...]) * (MLA_SCALE * LOG2_E)
    kva = _dot(ckv, wkv_ref[...])
    k_rope = _rope_pair(small[:, LANE:], cs).astype(BF16)
    for h in range(MLA_HEADS):
        base = h * MLA_QK_PAD
        mq_ref[0, h, :, :MLA_NOPE] = qa[:, base:base + MLA_NOPE].astype(BF16)
        mq_ref[0, h, :, MLA_NOPE:] = _rope_pair(qa[:, base + MLA_NOPE:base + MLA_QK_PAD], cs).astype(BF16)
        mk_ref[0, h, :, :MLA_NOPE] = kva[:, base:base + MLA_NOPE].astype(BF16)
        mk_ref[0, h, :, MLA_NOPE:] = k_rope
        mv_ref[0, h, :, :MLA_V] = kva[:, base + MLA_NOPE:base + MLA_NOPE + MLA_V].astype(BF16)
        mv_ref[0, h, :, MLA_V:] = jnp.ones((tm, MLA_V_PAD - MLA_V), BF16)


def _mix(h3, cs, lp):
    B, L, _ = h3.shape
    tm = TM_MIX
    cur, prev, nxt = _seq_specs(tm, D_MODEL, L)
    out = lambda w: pl.BlockSpec((1, tm, w), lambda b, i: (b, i, 0))
    head_out = lambda w: pl.BlockSpec((1, MLA_HEADS, tm, w), lambda b, i: (b, 0, i, 0))
    seq_shape = lambda w, dt: jax.ShapeDtypeStruct((B, L, w), dt)
    head_shape = lambda w: jax.ShapeDtypeStruct((B, MLA_HEADS, L, w), BF16)
    consts = (lp["mix_norm"], lp["w_conv"], lp["w_rest"], lp["w_small"],
              lp["gdn_conv"], lp["gdn_alog"], lp["gdn_bias"],
              lp["ssd_conv"], lp["ssd_conv_b"], lp["ssd_alog"], lp["ssd_bias"],
              lp["mla_q_norm"], lp["mla_kv_norm"], lp["mla_wq"], lp["mla_wkv"])
    return pl.pallas_call(
        _mix_kernel,
        grid=(B, L // tm),
        in_specs=[cur, prev, nxt, pl.BlockSpec((tm, LANE), lambda b, i: (i, 0))]
        + [_const_spec(a.shape) for a in consts],
        out_specs=[out(GDN_QK_W), out(GDN_QK_W), out(GDN_V_W), out(LANE), out(GDN_V_W), out(SSD_INNER),
                   out(SSD_XBC), out(LANE), out(LANE),
                   head_out(MLA_QK_PAD), head_out(MLA_QK_PAD), head_out(MLA_V_PAD)],
        out_shape=[seq_shape(GDN_QK_W, BF16), seq_shape(GDN_QK_W, BF16), seq_shape(GDN_V_W, BF16),
                   seq_shape(LANE, F32), seq_shape(GDN_V_W, BF16), seq_shape(SSD_INNER, BF16),
                   seq_shape(SSD_XBC, BF16), seq_shape(LANE, F32), seq_shape(LANE, F32),
                   head_shape(MLA_QK_PAD), head_shape(MLA_QK_PAD), head_shape(MLA_V_PAD)],
        compiler_params=_cparams("parallel", "parallel"),
        name="mix",
    )(h3, h3, h3, cs, *consts)


def _seq_specs(tr, width, L):
    per = tr // SUBLANE
    last = L // SUBLANE - 1
    cur = pl.BlockSpec((1, tr, width), lambda b, i: (b, i, 0))
    prev = pl.BlockSpec((1, SUBLANE, width), lambda b, i: (b, jnp.maximum(i * per - 1, 0), 0))
    nxt = pl.BlockSpec((1, SUBLANE, width), lambda b, i: (b, jnp.minimum((i + 1) * per, last), 0))
    return cur, prev, nxt


def _conv(xe, w_ref, c0, width):
    n = xe.shape[0]
    tr = n - 2 * SUBLANE
    acc = None
    for k in range(CONV_W):
        shifted = xe if k == CONV_HALF else pltpu.roll(xe, (CONV_HALF - k) % n, 0)
        term = shifted[SUBLANE:SUBLANE + tr] * w_ref[k:k + 1, c0:c0 + width]
        acc = term if acc is None else acc + term
    return acc


def _chunk_cumsum(g, lane_is_fwd):
    tr = g.shape[0]
    pos = lax.broadcasted_iota(jnp.int32, g.shape, 0) & (CHUNK - 1)
    pre = g
    suf = g
    s = 1
    while s < CHUNK:
        pre = pre + jnp.where(pos >= s, pltpu.roll(pre, s, 0), 0.0)
        suf = suf + jnp.where(pos < CHUNK - s, pltpu.roll(suf, tr - s, 0), 0.0)
        s *= 2
    return jnp.where(lane_is_fwd, pre, suf)


NEUMANN_STEPS = 6
assert 2 ** NEUMANN_STEPS == CHUNK


def _gdn_scan_kernel(qf_ref, kf_ref, vf_ref, gbf_ref, grf_ref, qb_ref, kb_ref, vb_ref, gbb_ref, grb_ref,
                     of_ref, ob_ref, s_ref, *, nc):
    @pl.when(pl.program_id(1) == 0)
    def _():
        s_ref[...] = jnp.zeros_like(s_ref)

    dirs = ((qf_ref, kf_ref, vf_ref, gbf_ref, grf_ref, of_ref),
            (qb_ref, kb_ref, vb_ref, gbb_ref, grb_ref, ob_ref))
    ii = lax.broadcasted_iota(jnp.int32, (CHUNK, CHUNK), 0)
    jj = lax.broadcasted_iota(jnp.int32, (CHUNK, CHUNK), 1)

    units = []
    for step_idx in range(nc):
        for d, (q_ref, k_ref, v_ref, gb_ref, gr_ref, _) in enumerate(dirs):
            cc = step_idx if d == 0 else nc - 1 - step_idx
            rows = slice(cc * CHUNK, (cc + 1) * CHUNK)
            gb = gb_ref[0, rows, :]
            grow_all = gr_ref[0, cc]
            for h in range(GDN_HEADS):
                j = d * GDN_HEADS + h
                sl = slice(h * GDN_DK, (h + 1) * GDN_DK)
                units.append(dict(d=d, j=j, rows=rows, sl=sl, q=q_ref[0, rows, sl], k=k_ref[0, rows, sl],
                                  v=v_ref[0, rows, sl], gcol=gb[:, j:j + 1], grow=grow_all[j:j + 1, :],
                                  bcol=gb[:, GDN_CHAINS + j:GDN_CHAINS + j + 1]))
    for t in units:
        r = _dot_nt(jnp.concatenate([t["k"], t["q"]], axis=0), t["k"])
        t["kk"] = r[:CHUNK]
        t["qk"] = r[CHUNK:]
    for t in units:
        gcol, bcol = t["gcol"], t["bcol"]
        incl = (ii >= jj) if t["d"] == 0 else (ii <= jj)
        dec = jnp.where(incl, jnp.exp(jnp.minimum(gcol - t["grow"], 0.0)), 0.0)
        t["n"] = -(bcol * t.pop("kk")) * jnp.where(ii == jj, 0.0, dec)
        t["attn"] = (t.pop("qk") * dec).astype(BF16)
        kf = t["k"].astype(F32)
        eg = jnp.exp(gcol)
        glast = gcol[CHUNK - 1:CHUNK] if t["d"] == 0 else gcol[0:1]
        t["x"] = bcol * jnp.concatenate([t["v"].astype(F32), kf * eg], axis=1)
        t["qd"] = t["q"].astype(F32) * eg
        t["kd"] = (kf * jnp.exp(glast - gcol)).astype(BF16)
        t["eglast"] = jnp.exp(glast)
    for step in range(NEUMANN_STEPS):
        for t in units:
            nb = t["n"].astype(BF16)
            t["y"] = _dot(nb, t["x"].astype(BF16))
            if step + 1 < NEUMANN_STEPS:
                t["n"] = _dot(nb, nb)
        for t in units:
            t["x"] = t["x"] + t.pop("y")
    for t in units:
        x = t.pop("x")
        t["u"] = x[:, :GDN_DV]
        t["wq"] = jnp.concatenate([x[:, GDN_DV:], t.pop("qd")], axis=0).astype(BF16)

    state = [s_ref[j] for j in range(GDN_CHAINS)]
    for step_idx in range(nc):
        group = units[step_idx * GDN_CHAINS:(step_idx + 1) * GDN_CHAINS]
        for t in group:
            t["r"] = _dot(t["wq"], state[t["j"]].astype(BF16))
        for t in group:
            t["vnew"] = (t["u"] - t["r"][:CHUNK]).astype(BF16)
        for t in group:
            t["o"] = t["r"][CHUNK:] + _dot(t["attn"], t["vnew"])
            state[t["j"]] = state[t["j"]] * t["eglast"] + _dot_tn(t["kd"], t["vnew"])
        for t in group:
            dirs[t["d"]][5][0, t["rows"], t["sl"]] = t["o"]
    for j in range(GDN_CHAINS):
        s_ref[j] = state[j]


def _gdn_scan(q, k, v, gb, grow):
    B, L, _ = q.shape
    tr = TR_SEQ
    nc = tr // CHUNK
    nt = L // tr
    fwd = lambda w: pl.BlockSpec((1, tr, w), lambda b, i: (b, i, 0))
    bwd = lambda w: pl.BlockSpec((1, tr, w), lambda b, i: (b, nt - 1 - i, 0))
    grf = pl.BlockSpec((1, nc, GDN_CHAINS, CHUNK), lambda b, i: (b, i, 0, 0))
    grb = pl.BlockSpec((1, nc, GDN_CHAINS, CHUNK), lambda b, i: (b, nt - 1 - i, 0, 0))
    return pl.pallas_call(
        functools.partial(_gdn_scan_kernel, nc=nc),
        grid=(B, nt),
        in_specs=[fwd(GDN_QK_W), fwd(GDN_QK_W), fwd(GDN_V_W), fwd(LANE), grf,
                  bwd(GDN_QK_W), bwd(GDN_QK_W), bwd(GDN_V_W), bwd(LANE), grb],
        out_specs=[fwd(GDN_V_W), bwd(GDN_V_W)],
        out_shape=[jax.ShapeDtypeStruct((B, L, GDN_V_W), F32)] * 2,
        scratch_shapes=[pltpu.VMEM((GDN_CHAINS, GDN_DK, GDN_DV), F32)],
        compiler_params=_cparams("arbitrary", "arbitrary"),
        name="gdn_scan",
    )(q, k, v, gb, grow, q, k, v, gb, grow)


def _expand_heads(cols, base):
    rows = cols.shape[0]
    low_half = lax.broadcasted_iota(jnp.int32, (rows, LANE), 1) < SSD_HEAD_DIM
    tiles = []
    for h in range(0, SSD_HEADS, LANE // SSD_HEAD_DIM):
        tiles.append(jnp.where(low_half, cols[:, base + h:base + h + 1], cols[:, base + h + 1:base + h + 2]))
    return jnp.concatenate(tiles, axis=1)


def _ssd_scan_kernel(xf_ref, dtf_ref, acf_ref, arf_ref, xb_ref, dtb_ref, acb_ref, arb_ref,
                     yf_ref, yb_ref, h_ref, *, nc):
    @pl.when(pl.program_id(1) == 0)
    def _():
        h_ref[...] = jnp.zeros_like(h_ref)

    dirs = ((xf_ref, dtf_ref, acf_ref, arf_ref, yf_ref), (xb_ref, dtb_ref, acb_ref, arb_ref, yb_ref))
    ii = lax.broadcasted_iota(jnp.int32, (CHUNK, SSD_GROUP_W), 0)
    jj = lax.broadcasted_iota(jnp.int32, (CHUNK, SSD_GROUP_W), 1) & (CHUNK - 1)
    rb = lax.broadcasted_iota(jnp.int32, (SSD_GROUP_W, SSD_GROUP_W), 0) >> HEAD_SHIFT
    cb_ = lax.broadcasted_iota(jnp.int32, (SSD_GROUP_W, SSD_GROUP_W), 1) >> HEAD_SHIFT
    block_diag = rb == cb_

    units = []
    for step_idx in range(nc):
        for d, (x_ref, dt_ref, ac_ref, ar_ref, _) in enumerate(dirs):
            cc = step_idx if d == 0 else nc - 1 - step_idx
            rows = slice(cc * CHUNK, (cc + 1) * CHUNK)
            base = SSD_DT_LANE + d * SSD_HEADS
            acx = _expand_heads(ac_ref[0, rows, :], base)
            xr = x_ref[0, rows, :SSD_INNER].astype(F32) * _expand_heads(dt_ref[0, rows, :], base)
            aclast = acx[CHUNK - 1:CHUNK] if d == 0 else acx[0:1]
            acrow = ar_ref[0, cc]
            for g in range(SSD_GROUPS):
                sl = slice(g * SSD_GROUP_W, (g + 1) * SSD_GROUP_W)
                b0 = SSD_INNER + g * SSD_STATE
                c0 = SSD_INNER + (SSD_GROUPS + g) * SSD_STATE
                units.append(dict(d=d, g=g, rows=rows, sl=sl, bg=x_ref[0, rows, b0:b0 + SSD_STATE],
                                  cg=x_ref[0, rows, c0:c0 + SSD_STATE], acx=acx[:, sl], acrow=acrow[:, sl],
                                  aclast=aclast[:, sl], xr=xr[:, sl]))
    for t in units:
        t["cbt"] = _dot_nt(t["cg"], jnp.concatenate([t["bg"]] * SSD_HEADS_PER_GROUP, axis=0))
    for t in units:
        incl = (ii >= jj) if t["d"] == 0 else (ii <= jj)
        seg = jnp.where(incl, jnp.exp(jnp.minimum(t["acx"] - t["acrow"], 0.0)), 0.0)
        m = (t.pop("cbt") * seg).astype(BF16)
        xr = t.pop("xr")
        xr_bd = jnp.where(block_diag, jnp.concatenate([xr] * SSD_HEADS_PER_GROUP, axis=0), 0.0).astype(BF16)
        t["ydiag"] = _dot(m, xr_bd)
        t["st"] = _dot_tn(t["bg"], (xr * jnp.exp(t["aclast"] - t["acx"])).astype(BF16))

    state = {(d, g): h_ref[d, g] for d in range(2) for g in range(SSD_GROUPS)}
    per_step = 2 * SSD_GROUPS
    for step_idx in range(nc):
        for t in units[step_idx * per_step:(step_idx + 1) * per_step]:
            key = (t["d"], t["g"])
            y = t["ydiag"] + _dot(t["cg"], state[key].astype(BF16)) * jnp.exp(t["acx"])
            dirs[t["d"]][4][0, t["rows"], t["sl"]] = y
            state[key] = state[key] * jnp.exp(t["aclast"]) + t["st"]
    for (d, g), hval in state.items():
        h_ref[d, g] = hval


def _ssd_scan(xbc, dt, ac, acrow_f, acrow_b):
    B, L, _ = xbc.shape
    tr = TR_SEQ
    nc = tr // CHUNK
    nt = L // tr
    fwd = lambda w: pl.BlockSpec((1, tr, w), lambda b, i: (b, i, 0))
    bwd = lambda w: pl.BlockSpec((1, tr, w), lambda b, i: (b, nt - 1 - i, 0))
    arf = pl.BlockSpec((1, nc, 1, SSD_INNER), lambda b, i: (b, i, 0, 0))
    arb = pl.BlockSpec((1, nc, 1, SSD_INNER), lambda b, i: (b, nt - 1 - i, 0, 0))
    return pl.pallas_call(
        functools.partial(_ssd_scan_kernel, nc=nc),
        grid=(B, nt),
        in_specs=[fwd(SSD_XBC), fwd(LANE), fwd(LANE), arf, bwd(SSD_XBC), bwd(LANE), bwd(LANE), arb],
        out_specs=[fwd(SSD_INNER), bwd(SSD_INNER)],
        out_shape=[jax.ShapeDtypeStruct((B, L, SSD_INNER), F32)] * 2,
        scratch_shapes=[pltpu.VMEM((2, SSD_GROUPS, SSD_STATE, SSD_GROUP_W), F32)],
        compiler_params=_cparams("arbitrary", "arbitrary"),
        name="ssd_scan",
    )(xbc, dt, ac, acrow_f, xbc, dt, ac, acrow_b)


def _rope_pair(pair, cs):
    prod = pair * cs
    lane = lax.broadcasted_iota(jnp.int32, prod.shape, 1)
    return jnp.where(lane < MLA_ROPE, prod + pltpu.roll(prod, MLA_ROPE, 1), 0.0)


def _mla_attn_kernel(q_ref, k_ref, v_ref, o_ref):
    hs, tq = q_ref.shape[1], q_ref.shape[2]
    nk = k_ref.shape[2] // TK_ATTN
    steps = [(h, j) for h in range(hs) for j in range(nk)]

    def scores(h, j):
        return _dot_nt(q_ref[0, h], k_ref[0, h, j * TK_ATTN:(j + 1) * TK_ATTN, :])

    s_next = scores(*steps[0])
    for idx, (h, j) in enumerate(steps):
        s = s_next
        if idx + 1 < len(steps):
            s_next = scores(*steps[idx + 1])
        if j == 0:
            m = jnp.full((tq, 1), -jnp.inf, F32)
            acc = jnp.zeros((tq, MLA_V_PAD), F32)
        m_new = jnp.maximum(m, jnp.max(s, axis=-1, keepdims=True))
        p = jnp.exp2(s - m_new).astype(BF16)
        acc = jnp.exp2(m - m_new) * acc + _dot(p, v_ref[0, h, j * TK_ATTN:(j + 1) * TK_ATTN, :])
        m = m_new
        if j == nk - 1:
            o_ref[0, :, h * MLA_V:(h + 1) * MLA_V] = (acc[:, :MLA_V] / acc[:, MLA_V:]).astype(BF16)


def _mla_attn(q, k, v):
    B, H, L, _ = q.shape
    tq = TQ_ATTN
    kv_bytes_per_head = 2 * L * (MLA_QK_PAD + MLA_V_PAD) * 2
    hs = max(1, min(H, ATTN_KV_VMEM_BUDGET // kv_bytes_per_head))
    while H % hs:
        hs -= 1
    return pl.pallas_call(
        _mla_attn_kernel,
        grid=(B, H // hs, L // tq),
        in_specs=[pl.BlockSpec((1, hs, tq, MLA_QK_PAD), lambda b, h, i: (b, h, i, 0)),
                  pl.BlockSpec((1, hs, L, MLA_QK_PAD), lambda b, h, i: (b, h, 0, 0)),
                  pl.BlockSpec((1, hs, L, MLA_V_PAD), lambda b, h, i: (b, h, 0, 0))],
        out_specs=pl.BlockSpec((1, tq, hs * MLA_V), lambda b, h, i: (b, i, h)),
        out_shape=jax.ShapeDtypeStruct((B, L, H * MLA_V), BF16),
        compiler_params=_cparams("parallel", "parallel", "parallel"),
        name="mla_attn",
    )(q, k, v)


def _merge_kernel(h_ref, of_ref, ob_ref, za_ref, yf_ref, yb_ref, xbc_ref, zb_ref, oc_ref,
                  mixg_ref, gng_ref, dskip_ref, sng_ref, wa_ref, wb_ref, wc_ref, wg_ref, bg_ref, wo_ref,
                  out_ref):
    h = h_ref[...]
    u = _rms(h, mixg_ref[...]).astype(BF16)
    gates = jax.nn.sigmoid(_dot(u, wg_ref[...]) + bg_ref[...])

    o = of_ref[...] + ob_ref[...]
    za = za_ref[...].astype(F32)
    parts = []
    for hd in range(GDN_HEADS):
        sl = slice(hd * GDN_DV, (hd + 1) * GDN_DV)
        parts.append(_rms(o[:, sl], gng_ref[...]) * _silu(za[:, sl]))
    ya = _dot(jnp.concatenate(parts, axis=1).astype(BF16), wa_ref[...])

    y = yf_ref[...] + yb_ref[...] + xbc_ref[...].astype(F32) * dskip_ref[...]
    y = _rms(y * _silu(zb_ref[...].astype(F32)), sng_ref[...])
    yb = _dot(y.astype(BF16), wb_ref[...])

    yc = _dot(oc_ref[...], wc_ref[...])

    merged = (gates[:, :D_MODEL] * ya + gates[:, D_MODEL:2 * D_MODEL] * yb
              + gates[:, 2 * D_MODEL:] * yc)
    out_ref[...] = h + _dot(merged.astype(BF16), wo_ref[...])


def _merge(h2d, o_f, o_b, za, y_f, y_b, xbc_act, zb, oc, mix_gain, gdn_gain, dskip, ssd_gain,
           wa, wb, wc, wg, bg, wo):
    T = h2d.shape[0]
    tm = TM_MERGE
    row = lambda w: pl.BlockSpec((tm, w), lambda i: (i, 0))
    return pl.pallas_call(
        _merge_kernel,
        grid=(T // tm,),
        in_specs=[row(D_MODEL), row(GDN_V_W), row(GDN_V_W), row(GDN_V_W), row(SSD_INNER), row(SSD_INNER),
                  row(SSD_INNER), row(SSD_INNER), row(MLA_HEADS * MLA_V),
                  _const_spec((1, D_MODEL)), _const_spec((1, GDN_DV)), _const_spec((1, SSD_INNER)),
                  _const_spec((1, SSD_INNER)), _const_spec((GDN_V_W, D_MODEL)),
                  _const_spec((SSD_INNER, D_MODEL)), _const_spec((MLA_HEADS * MLA_V, D_MODEL)),
                  _const_spec((D_MODEL, N_BRANCH * D_MODEL)), _const_spec((1, N_BRANCH * D_MODEL)),
                  _const_spec((D_MODEL, D_MODEL))],
        out_specs=row(D_MODEL),
        out_shape=jax.ShapeDtypeStruct((T, D_MODEL), F32),
        compiler_params=_cparams("parallel"),
        name="merge",
    )(h2d, o_f, o_b, za, y_f, y_b, xbc_act, zb, oc, mix_gain, gdn_gain, dskip, ssd_gain,
      wa, wb, wc, wg, bg, wo)


def _rotate_half_cols(w):
    w1, w2 = jnp.split(w, 2, axis=-1)
    return jnp.concatenate([-w2, w1], axis=-1)


def _lane_row(values, base):
    row = jnp.zeros((1, LANE), F32)
    return row.at[0, base:base + values.shape[0]].set(values.astype(F32))


def _prep_layer(p, i):
    w_in = p["w_in"][i]
    c = 0
    seg = {}
    for name, width in (("qkv", GDN_QKV_W), ("za", GDN_V_W), ("ab", 4 * GDN_HEADS), ("zb", SSD_INNER),
                        ("xbc", SSD_XBC), ("dt", 2 * SSD_HEADS), ("cq", MLA_Q_LORA), ("ckv", MLA_KV_LORA),
                        ("kr", MLA_ROPE)):
        seg[name] = w_in[:, c:c + width]
        c += width
    w_conv = jnp.concatenate([seg["qkv"], seg["xbc"]], axis=1)
    w_rest = jnp.concatenate([seg["za"], seg["zb"], seg["cq"], seg["ckv"]], axis=1)
    pad = jnp.zeros((D_MODEL, LANE - 4 * GDN_HEADS - 2 * SSD_HEADS), F32)
    w_small = jnp.concatenate([seg["ab"], seg["dt"], pad, seg["kr"], _rotate_half_cols(seg["kr"])], axis=1)

    wq = p["mla_w_uq"][i].reshape(MLA_Q_LORA, MLA_HEADS, MLA_NOPE + MLA_ROPE)
    wq_rope = wq[..., MLA_NOPE:]
    wq = jnp.concatenate([wq, _rotate_half_cols(wq_rope)], axis=-1).reshape(MLA_Q_LORA, MLA_HEADS * MLA_QK_PAD)

    row = lambda v: v.reshape(1, -1).astype(F32)
    conv_pad = lambda w: jnp.concatenate([w, jnp.zeros((SUBLANE - CONV_W, w.shape[1]), F32)], axis=0)
    return dict(
        ffn1_norm=row(p["ffn1_norm"][i]), w_ffn1_in=p["w_ffn1_in"][i].astype(BF16),
        w_ffn1_out=p["w_ffn1_out"][i].astype(BF16),
        mix_norm=row(p["mix_norm"][i]), w_conv=w_conv.astype(BF16), w_rest=w_rest.astype(BF16),
        w_small=w_small.astype(BF16),
        gdn_conv=conv_pad(p["gdn_conv"][i]),
        gdn_alog=_lane_row(p["gdn_A_log"][i].reshape(-1), GDN_AB_LANE),
        gdn_bias=_lane_row(p["gdn_dt_bias"][i].reshape(-1), GDN_AB_LANE),
        gdn_norm=row(p["gdn_norm"][i]),
        ssd_conv=conv_pad(p["ssd_conv"][i]), ssd_conv_b=row(p["ssd_conv_b"][i]),
        ssd_alog=_lane_row(p["ssd_A_log"][i].reshape(-1), SSD_DT_LANE),
        ssd_bias=_lane_row(p["ssd_dt_bias"][i].reshape(-1), SSD_DT_LANE),
        ssd_dskip=row(jnp.repeat(p["ssd_D"][i], SSD_HEAD_DIM)), ssd_norm=row(p["ssd_norm"][i]),
        mla_q_norm=row(p["mla_q_norm"][i]), mla_kv_norm=row(p["mla_kv_norm"][i]),
        mla_wq=wq.astype(BF16), mla_wkv=p["mla_w_ukv"][i].astype(BF16),
        w_branch_a=p["w_branch_a"][i].astype(BF16), w_branch_b=p["w_branch_b"][i].astype(BF16),
        w_branch_c=p["w_branch_c"][i].astype(BF16), w_gate=p["w_gate"][i].astype(BF16),
        b_gate=row(p["b_gate"][i]), w_out=p["w_out"][i].astype(BF16),
        ffn2_norm=row(p["ffn2_norm"][i]), w_ffn2_in=p["w_ffn2_in"][i].astype(BF16),
        w_ffn2_out=p["w_ffn2_out"][i].astype(BF16),
    )


def _rope_table(L):
    inv_freq = jnp.power(ROPE_BASE, -jnp.arange(0, MLA_ROPE, 2, dtype=F32) / MLA_ROPE)
    ang = jnp.arange(L, dtype=F32)[:, None] * inv_freq[None, :]
    ang = jnp.concatenate([ang, ang], axis=-1)
    return jnp.concatenate([jnp.cos(ang), jnp.sin(ang)], axis=-1)


def _rows_to_chunk_rows(cols, B, L):
    n = cols.shape[-1]
    return cols.reshape(B, L // CHUNK, CHUNK, n).transpose(0, 1, 3, 2)


def _layer(x2d, B, L, cs, lp, final_gain, final):
    T = B * L
    h = _ffn(x2d, lp["ffn1_norm"], lp["w_ffn1_in"], lp["w_ffn1_out"], final_gain, False)
    q, k, v, gb, za, zb, xbc_act, dt, ac, qc, kc, vc = _mix(h.reshape(B, L, D_MODEL), cs, lp)

    grow = _rows_to_chunk_rows(gb[..., :GDN_CHAINS], B, L)
    o_f, o_b = _gdn_scan(q, k, v, gb, grow)

    acrow = _rows_to_chunk_rows(ac[..., SSD_DT_LANE:SSD_DT_LANE + 2 * SSD_HEADS], B, L)
    acrow = acrow.reshape(B, L // CHUNK, 2, 1, SSD_HEADS, 1, CHUNK)
    acrow = jnp.broadcast_to(acrow, (B, L // CHUNK, 2, 1, SSD_HEADS, SSD_HEAD_DIM // CHUNK, CHUNK))
    acrow = acrow.reshape(B, L // CHUNK, 2, 1, SSD_INNER)
    y_f, y_b = _ssd_scan(xbc_act, dt, ac, acrow[:, :, 0], acrow[:, :, 1])

    oc = _mla_attn(qc, kc, vc)

    flat = lambda a: a.reshape(T, a.shape[-1])
    h = _merge(h, flat(o_f), flat(o_b), flat(za), flat(y_f), flat(y_b), flat(xbc_act), flat(zb), flat(oc),
               lp["mix_norm"], lp["gdn_norm"], lp["ssd_dskip"], lp["ssd_norm"],
               lp["w_branch_a"], lp["w_branch_b"], lp["w_branch_c"], lp["w_gate"], lp["b_gate"], lp["w_out"])
    return _ffn(h, lp["ffn2_norm"], lp["w_ffn2_in"], lp["w_ffn2_out"], final_gain, final)


def _trunk(x, layers, final_gain):
    B, L, _ = x.shape
    cs = _rope_table(L)
    h = x.reshape(B * L, D_MODEL)
    for i, lp in enumerate(layers):
        h = _layer(h, B, L, cs, lp, final_gain, i == len(layers) - 1)
    return h.reshape(B, L, D_MODEL)


def kernel(x_prompt, x_sample, ffn1_norm, w_ffn1_in, w_ffn1_out, mix_norm, w_in, gdn_conv, gdn_A_log, gdn_dt_bias, gdn_norm, ssd_conv, ssd_conv_b, ssd_A_log, ssd_dt_bias, ssd_D, ssd_norm, mla_q_norm, mla_w_uq, mla_kv_norm, mla_w_ukv, w_branch_a, w_branch_b, w_branch_c, w_gate, b_gate, w_out, ffn2_norm, w_ffn2_in, w_ffn2_out, final_norm):
    p = dict(ffn1_norm=ffn1_norm, w_ffn1_in=w_ffn1_in, w_ffn1_out=w_ffn1_out, mix_norm=mix_norm, w_in=w_in,
             gdn_conv=gdn_conv, gdn_A_log=gdn_A_log, gdn_dt_bias=gdn_dt_bias, gdn_norm=gdn_norm,
             ssd_conv=ssd_conv, ssd_conv_b=ssd_conv_b, ssd_A_log=ssd_A_log, ssd_dt_bias=ssd_dt_bias,
             ssd_D=ssd_D, ssd_norm=ssd_norm, mla_q_norm=mla_q_norm, mla_w_uq=mla_w_uq,
             mla_kv_norm=mla_kv_norm, mla_w_ukv=mla_w_ukv, w_branch_a=w_branch_a, w_branch_b=w_branch_b,
             w_branch_c=w_branch_c, w_gate=w_gate, b_gate=b_gate, w_out=w_out, ffn2_norm=ffn2_norm,
             w_ffn2_in=w_ffn2_in, w_ffn2_out=w_ffn2_out)
    layers = [_prep_layer(p, i) for i in range(DEPTH)]
    final_gain = final_norm.reshape(1, D_MODEL).astype(F32)
    return (_trunk(x_prompt, layers, final_gain), _trunk(x_sample, layers, final_gain))
```

```python
import functools

import jax
import jax.numpy as jnp
from jax import lax
from jax.experimental import pallas as pl
from jax.experimental.pallas import tpu as pltpu

F32 = jnp.float32
BF16 = jnp.bfloat16

D_MODEL = 1024
DEPTH = 2
EPS = 1e-6
CONV_W = 5
CONV_HALF = CONV_W // 2
CHUNK = 64

GDN_HEADS = 4
GDN_DK = 128
GDN_DV = 128
GDN_QK_W = GDN_HEADS * GDN_DK
GDN_V_W = GDN_HEADS * GDN_DV
GDN_QKV_W = 2 * GDN_QK_W + GDN_V_W
GDN_CHAINS = 2 * GDN_HEADS

SSD_HEADS = 8
SSD_HEAD_DIM = 64
SSD_INNER = SSD_HEADS * SSD_HEAD_DIM
SSD_GROUPS = 2
SSD_STATE = 128
SSD_XBC = SSD_INNER + 2 * SSD_GROUPS * SSD_STATE
SSD_GROUP_W = SSD_INNER // SSD_GROUPS
SSD_HEADS_PER_GROUP = SSD_HEADS // SSD_GROUPS
HEAD_SHIFT = 6
assert (1 << HEAD_SHIFT) == SSD_HEAD_DIM == CHUNK

MLA_HEADS = 4
MLA_Q_LORA = 256
MLA_KV_LORA = 256
MLA_NOPE = 128
MLA_ROPE = 64
MLA_V = 128
MLA_QK_PAD = 256
MLA_SCALE = (MLA_NOPE + MLA_ROPE) ** -0.5
MLA_V_PAD = 256
LOG2_E = 1.4426950408889634
ROPE_BASE = 10000.0

FFN_HIDDEN = 2816
N_BRANCH = 3

CONV_GROUP_W = 512
SMALL_W = 256
LANE = 128
SUBLANE = 8
GDN_AB_LANE = 0
SSD_DT_LANE = 16

VMEM_LIMIT = 56 * 1024 * 1024

TM_FFN = 512
TM_MIX = 512
TM_MERGE = 512
TR_SEQ = 512
TQ_ATTN = 512
TK_ATTN = 512
ATTN_KV_VMEM_BUDGET = 32 * 1024 * 1024


def _cparams(*sem):
    return pltpu.CompilerParams(dimension_semantics=sem, vmem_limit_bytes=VMEM_LIMIT)


def _const_spec(shape):
    nd = len(shape)
    return pl.BlockSpec(shape, lambda *_: (0,) * nd, pipeline_mode=pl.Buffered(1))


def _rms(x, gain):
    return x * lax.rsqrt(jnp.mean(x * x, axis=-1, keepdims=True) + EPS) * gain


def _silu(x):
    t = 0.5 * x
    return t + t * jnp.tanh(t)


def _softplus(x):
    return jnp.maximum(x, 0.0) + jnp.log1p(jnp.exp(-jnp.abs(x)))


def _dot(a, b):
    return jnp.dot(a, b, preferred_element_type=F32)


def _dot_nt(a, b):
    return lax.dot_general(a, b, (((1,), (1,)), ((), ())), preferred_element_type=F32)


def _dot_tn(a, b):
    return lax.dot_general(a, b, (((0,), (0,)), ((), ())), preferred_element_type=F32)


def _ffn_kernel(x_ref, g_ref, win_ref, wout_ref, fin_ref, o_ref, *, final):
    x = x_ref[...]
    xn = _rms(x, g_ref[...]).astype(BF16)
    gu = _dot(xn, win_ref[...])
    act = (_silu(gu[:, :FFN_HIDDEN]) * gu[:, FFN_HIDDEN:]).astype(BF16)
    h = x + 0.5 * _dot(act, wout_ref[...])
    if final:
        h = _rms(h, fin_ref[...])
    o_ref[...] = h


def _ffn(x2d, gain, w_in, w_out, fin, final):
    T = x2d.shape[0]
    tm = TM_FFN
    row = pl.BlockSpec((tm, D_MODEL), lambda i: (i, 0))
    return pl.pallas_call(
        functools.partial(_ffn_kernel, final=final),
        grid=(T // tm,),
        in_specs=[row, _const_spec((1, D_MODEL)), _const_spec((D_MODEL, 2 * FFN_HIDDEN)),
                  _const_spec((FFN_HIDDEN, D_MODEL)), _const_spec((1, D_MODEL))],
        out_specs=row,
        out_shape=jax.ShapeDtypeStruct((T, D_MODEL), F32),
        compiler_params=_cparams("parallel"),
        name="ffn",
    )(x2d, gain, w_in, w_out, fin)


def _mix_kernel(cur_ref, prev_ref, next_ref, cs_ref, g_ref, wconv_ref, wrest_ref, wsmall_ref,
                gconv_ref, galog_ref, gbias_ref, sconv_ref, sconvb_ref, salog_ref, sbias_ref,
                qn_ref, kvn_ref, wq_ref, wkv_ref,
                q_ref, k_ref, v_ref, gb_ref, za_ref, zb_ref, xbc_ref, dt_ref, ac_ref,
                mq_ref, mk_ref, mv_ref):
    i = pl.program_id(1)
    tm = cur_ref.shape[1]
    body = slice(SUBLANE, SUBLANE + tm)
    first = i == 0
    last = i == pl.num_programs(1) - 1
    hx = jnp.concatenate([prev_ref[0], cur_ref[0], next_ref[0]], axis=0)
    u = _rms(hx, g_ref[...]).astype(BF16)

    def conv_cols(c0, w_ref, wc0):
        cols = _dot(u, wconv_ref[:, c0:c0 + CONV_GROUP_W])
        xe = jnp.concatenate([jnp.where(first, 0.0, cols[:SUBLANE]), cols[body],
                              jnp.where(last, 0.0, cols[SUBLANE + tm:])], axis=0)
        return _conv(xe, w_ref, wc0, CONV_GROUP_W)

    for c0, o_ref, scale in ((0, q_ref, GDN_DK ** -0.5), (GDN_QK_W, k_ref, 1.0)):
        a = _silu(conv_cols(c0, gconv_ref, c0))
        for h in range(GDN_HEADS):
            sl = slice(h * GDN_DK, (h + 1) * GDN_DK)
            seg = a[:, sl]
            o_ref[0, :, sl] = (seg * (lax.rsqrt(jnp.sum(seg * seg, axis=-1, keepdims=True) + EPS)
                                      * scale)).astype(BF16)
    v_ref[0] = _silu(conv_cols(2 * GDN_QK_W, gconv_ref, 2 * GDN_QK_W)).astype(BF16)

    for c0 in range(0, SSD_XBC, CONV_GROUP_W):
        xbc_ref[0, :, c0:c0 + CONV_GROUP_W] = _silu(
            conv_cols(GDN_QKV_W + c0, sconv_ref, c0) + sconvb_ref[:, c0:c0 + CONV_GROUP_W]).astype(BF16)

    rest = _dot(u, wrest_ref[...])[body]
    za_ref[0] = rest[:, :GDN_V_W].astype(BF16)
    zb_ref[0] = rest[:, GDN_V_W:GDN_V_W + SSD_INNER].astype(BF16)
    small = _dot(u, wsmall_ref[...])[body]

    s = small[:, :LANE]
    lane = lax.broadcasted_iota(jnp.int32, s.shape, 1)
    is_gdn = lane < GDN_CHAINS
    is_ssd = (lane >= SSD_DT_LANE) & (lane < SSD_DT_LANE + 2 * SSD_HEADS)
    sp = _softplus(s + gbias_ref[...] + sbias_ref[...])
    log_decay = jnp.where(is_gdn | is_ssd, -jnp.exp(galog_ref[...] + salog_ref[...]) * sp, 0.0)
    fwd = (lane < GDN_HEADS) | ((lane >= SSD_DT_LANE) & (lane < SSD_DT_LANE + SSD_HEADS))
    csum = _chunk_cumsum(log_decay, fwd)
    gb_ref[0] = jnp.where(is_gdn, csum, jax.nn.sigmoid(s))
    dt_ref[0] = jnp.where(is_ssd, sp, 0.0)
    ac_ref[0] = jnp.where(is_ssd, csum, 0.0)

    c = rest[:, GDN_V_W + SSD_INNER:]
    cs = cs_ref[...]
    cq = _rms(c[:, :MLA_Q_LORA], qn_ref[...]).astype(BF16)
    ckv = _rms(c[:, MLA_Q_LORA:], kvn_ref[...]).astype(BF16)
    qa = _dot(cq, wq_ref[...]) * (MLA_SCALE * LOG2_E)
    kva = _dot(ckv, wkv_ref[...])
    k_rope = _rope_pair(small[:, LANE:], cs).astype(BF16)
    for h in range(MLA_HEADS):
        base = h * MLA_QK_PAD
        mq_ref[0, h, :, :MLA_NOPE] = qa[:, base:base + MLA_NOPE].astype(BF16)
        mq_ref[0, h, :, MLA_NOPE:] = _rope_pair(qa[:, base + MLA_NOPE:base + MLA_QK_PAD], cs).astype(BF16)
        mk_ref[0, h, :, :MLA_NOPE] = kva[:, base:base + MLA_NOPE].astype(BF16)
        mk_ref[0, h, :, MLA_NOPE:] = k_rope
        mv_ref[0, h, :, :MLA_V] = kva[:, base + MLA_NOPE:base + MLA_NOPE + MLA_V].astype(BF16)
        mv_ref[0, h, :, MLA_V:] = jnp.ones((tm, MLA_V_PAD - MLA_V), BF16)


def _mix(h3, cs, lp):
    B, L, _ = h3.shape
    tm = TM_MIX
    cur, prev, nxt = _seq_specs(tm, D_MODEL, L)
    out = lambda w: pl.BlockSpec((1, tm, w), lambda b, i: (b, i, 0))
    head_out = lambda w: pl.BlockSpec((1, MLA_HEADS, tm, w), lambda b, i: (b, 0, i, 0))
    seq_shape = lambda w, dt: jax.ShapeDtypeStruct((B, L, w), dt)
    head_shape = lambda w: jax.ShapeDtypeStruct((B, MLA_HEADS, L, w), BF16)
    consts = (lp["mix_norm"], lp["w_conv"], lp["w_rest"], lp["w_small"],
              lp["gdn_conv"], lp["gdn_alog"], lp["gdn_bias"],
              lp["ssd_conv"], lp["ssd_conv_b"], lp["ssd_alog"], lp["ssd_bias"],
              lp["mla_q_norm"], lp["mla_kv_norm"], lp["mla_wq"], lp["mla_wkv"])
    return pl.pallas_call(
        _mix_kernel,
        grid=(B, L // tm),
        in_specs=[cur, prev, nxt, pl.BlockSpec((tm, LANE), lambda b, i: (i, 0))]
        + [_const_spec(a.shape) for a in consts],
        out_specs=[out(GDN_QK_W), out(GDN_QK_W), out(GDN_V_W), out(LANE), out(GDN_V_W), out(SSD_INNER),
                   out(SSD_XBC), out(LANE), out(LANE),
                   head_out(MLA_QK_PAD), head_out(MLA_QK_PAD), head_out(MLA_V_PAD)],
        out_shape=[seq_shape(GDN_QK_W, BF16), seq_shape(GDN_QK_W, BF16), seq_shape(GDN_V_W, BF16),
                   seq_shape(LANE, F32), seq_shape(GDN_V_W, BF16), seq_shape(SSD_INNER, BF16),
                   seq_shape(SSD_XBC, BF16), seq_shape(LANE, F32), seq_shape(LANE, F32),
                   head_shape(MLA_QK_PAD), head_shape(MLA_QK_PAD), head_shape(MLA_V_PAD)],
        compiler_params=_cparams("parallel", "parallel"),
        name="mix",
    )(h3, h3, h3, cs, *consts)


def _seq_specs(tr, width, L):
    per = tr // SUBLANE
    last = L // SUBLANE - 1
    cur = pl.BlockSpec((1, tr, width), lambda b, i: (b, i, 0))
    prev = pl.BlockSpec((1, SUBLANE, width), lambda b, i: (b, jnp.maximum(i * per - 1, 0), 0))
    nxt = pl.BlockSpec((1, SUBLANE, width), lambda b, i: (b, jnp.minimum((i + 1) * per, last), 0))
    return cur, prev, nxt


def _conv(xe, w_ref, c0, width):
    n = xe.shape[0]
    groups = (n - 2 * SUBLANE) // SUBLANE
    x3 = xe.reshape(n // SUBLANE, SUBLANE, width)
    sub = lax.broadcasted_iota(jnp.int32, (groups, SUBLANE, width), 1)
    acc = None
    for k in range(CONV_W):
        s = CONV_HALF - k
        if s == 0:
            shifted = x3[1:1 + groups]
        else:
            r = pltpu.roll(x3, s % SUBLANE, 1)
            if s > 0:
                shifted = jnp.where(sub < s, r[0:groups], r[1:1 + groups])
            else:
                shifted = jnp.where(sub >= SUBLANE + s, r[2:2 + groups], r[1:1 + groups])
        term = shifted * w_ref[k:k + 1, c0:c0 + width]
        acc = term if acc is None else acc + term
    return acc.reshape(groups * SUBLANE, width)


def _chunk_cumsum(g, lane_is_fwd):
    tr = g.shape[0]
    pos = lax.broadcasted_iota(jnp.int32, g.shape, 0) & (CHUNK - 1)
    pre = g
    suf = g
    s = 1
    while s < CHUNK:
        pre = pre + jnp.where(pos >= s, pltpu.roll(pre, s, 0), 0.0)
        suf = suf + jnp.where(pos < CHUNK - s, pltpu.roll(suf, tr - s, 0), 0.0)
        s *= 2
    return jnp.where(lane_is_fwd, pre, suf)


NEUMANN_STEPS = 6
assert 2 ** NEUMANN_STEPS == CHUNK


def _gdn_scan_kernel(qf_ref, kf_ref, vf_ref, gbf_ref, grf_ref, qb_ref, kb_ref, vb_ref, gbb_ref, grb_ref,
                     of_ref, ob_ref, s_ref, *, nc):
    @pl.when(pl.program_id(1) == 0)
    def _():
        s_ref[...] = jnp.zeros_like(s_ref)

    dirs = ((qf_ref, kf_ref, vf_ref, gbf_ref, grf_ref, of_ref),
            (qb_ref, kb_ref, vb_ref, gbb_ref, grb_ref, ob_ref))
    ii = lax.broadcasted_iota(jnp.int32, (CHUNK, CHUNK), 0)
    jj = lax.broadcasted_iota(jnp.int32, (CHUNK, CHUNK), 1)
    eye = jnp.where(ii == jj, 1.0, 0.0)
    t_half = lax.broadcasted_iota(jnp.int32, (CHUNK, 2 * CHUNK), 1) >= CHUNK

    units = []
    for step_idx in range(nc):
        for d, (q_ref, k_ref, v_ref, gb_ref, gr_ref, _) in enumerate(dirs):
            cc = step_idx if d == 0 else nc - 1 - step_idx
            rows = slice(cc * CHUNK, (cc + 1) * CHUNK)
            gb = gb_ref[0, rows, :]
            grow_all = gr_ref[0, cc]
            for h in range(GDN_HEADS):
                j = d * GDN_HEADS + h
                sl = slice(h * GDN_DK, (h + 1) * GDN_DK)
                units.append(dict(d=d, j=j, rows=rows, sl=sl, q=q_ref[0, rows, sl], k=k_ref[0, rows, sl],
                                  v=v_ref[0, rows, sl], gcol=gb[:, j:j + 1], grow=grow_all[j:j + 1, :],
                                  bcol=gb[:, GDN_CHAINS + j:GDN_CHAINS + j + 1]))
    for t in units:
        r = _dot_nt(jnp.concatenate([t["k"], t["q"]], axis=0), t["k"])
        t["kk"] = r[:CHUNK]
        t["qk"] = r[CHUNK:]
    for t in units:
        gcol, bcol = t["gcol"], t["bcol"]
        incl = (ii >= jj) if t["d"] == 0 else (ii <= jj)
        dec = jnp.where(incl, jnp.exp(jnp.minimum(gcol - t["grow"], 0.0)), 0.0)
        nmat = -(bcol * t.pop("kk")) * jnp.where(ii == jj, 0.0, dec)
        t["nt"] = jnp.concatenate([nmat, eye], axis=1)
        t["attn"] = (t.pop("qk") * dec).astype(BF16)
        kf = t["k"].astype(F32)
        eg = jnp.exp(gcol)
        glast = gcol[CHUNK - 1:CHUNK] if t["d"] == 0 else gcol[0:1]
        t["rhs"] = (bcol * jnp.concatenate([t["v"].astype(F32), kf * eg], axis=1)).astype(BF16)
        t["qd"] = t["q"].astype(F32) * eg
        t["kd"] = (kf * jnp.exp(glast - gcol)).astype(BF16)
        t["eglast"] = jnp.exp(glast)
    for step in range(NEUMANN_STEPS):
        for t in units:
            ntb = t["nt"].astype(BF16)
            t["y"] = _dot(ntb[:, :CHUNK], ntb)
        for t in units:
            t["nt"] = t.pop("y") + jnp.where(t_half, t["nt"], 0.0)
    for t in units:
        x = _dot(t.pop("nt")[:, CHUNK:].astype(BF16), t.pop("rhs"))
        t["u"] = x[:, :GDN_DV]
        t["wq"] = jnp.concatenate([x[:, GDN_DV:], t.pop("qd")], axis=0).astype(BF16)

    state = [s_ref[j] for j in range(GDN_CHAINS)]
    for step_idx in range(nc):
        group = units[step_idx * GDN_CHAINS:(step_idx + 1) * GDN_CHAINS]
        for t in group:
            t["r"] = _dot(t["wq"], state[t["j"]].astype(BF16))
        for t in group:
            t["vnew"] = (t["u"] - t["r"][:CHUNK]).astype(BF16)
        for t in group:
            t["o"] = t["r"][CHUNK:] + _dot(t["attn"], t["vnew"])
            state[t["j"]] = state[t["j"]] * t["eglast"] + _dot_tn(t["kd"], t["vnew"])
        for t in group:
            dirs[t["d"]][5][0, t["rows"], t["sl"]] = t["o"]
    for j in range(GDN_CHAINS):
        s_ref[j] = state[j]


def _gdn_scan(q, k, v, gb, grow):
    B, L, _ = q.shape
    tr = TR_SEQ
    nc = tr // CHUNK
    nt = L // tr
    fwd = lambda w: pl.BlockSpec((1, tr, w), lambda b, i: (b, i, 0))
    bwd = lambda w: pl.BlockSpec((1, tr, w), lambda b, i: (b, nt - 1 - i, 0))
    grf = pl.BlockSpec((1, nc, GDN_CHAINS, CHUNK), lambda b, i: (b, i, 0, 0))
    grb = pl.BlockSpec((1, nc, GDN_CHAINS, CHUNK), lambda b, i: (b, nt - 1 - i, 0, 0))
    return pl.pallas_call(
        functools.partial(_gdn_scan_kernel, nc=nc),
        grid=(B, nt),
        in_specs=[fwd(GDN_QK_W), fwd(GDN_QK_W), fwd(GDN_V_W), fwd(LANE), grf,
                  bwd(GDN_QK_W), bwd(GDN_QK_W), bwd(GDN_V_W), bwd(LANE), grb],
        out_specs=[fwd(GDN_V_W), bwd(GDN_V_W)],
        out_shape=[jax.ShapeDtypeStruct((B, L, GDN_V_W), F32)] * 2,
        scratch_shapes=[pltpu.VMEM((GDN_CHAINS, GDN_DK, GDN_DV), F32)],
        compiler_params=_cparams("arbitrary", "arbitrary"),
        name="gdn_scan",
    )(q, k, v, gb, grow, q, k, v, gb, grow)


def _expand_heads(cols, base):
    rows = cols.shape[0]
    low_half = lax.broadcasted_iota(jnp.int32, (rows, LANE), 1) < SSD_HEAD_DIM
    tiles = []
    for h in range(0, SSD_HEADS, LANE // SSD_HEAD_DIM):
        tiles.append(jnp.where(low_half, cols[:, base + h:base + h + 1], cols[:, base + h + 1:base + h + 2]))
    return jnp.concatenate(tiles, axis=1)


def _ssd_scan_kernel(xf_ref, dtf_ref, acf_ref, arf_ref, xb_ref, dtb_ref, acb_ref, arb_ref,
                     yf_ref, yb_ref, h_ref, *, nc):
    @pl.when(pl.program_id(1) == 0)
    def _():
        h_ref[...] = jnp.zeros_like(h_ref)

    dirs = ((xf_ref, dtf_ref, acf_ref, arf_ref, yf_ref), (xb_ref, dtb_ref, acb_ref, arb_ref, yb_ref))
    ii = lax.broadcasted_iota(jnp.int32, (CHUNK, SSD_GROUP_W), 0)
    jj = lax.broadcasted_iota(jnp.int32, (CHUNK, SSD_GROUP_W), 1) & (CHUNK - 1)
    rb = lax.broadcasted_iota(jnp.int32, (SSD_GROUP_W, SSD_GROUP_W), 0) >> HEAD_SHIFT
    cb_ = lax.broadcasted_iota(jnp.int32, (SSD_GROUP_W, SSD_GROUP_W), 1) >> HEAD_SHIFT
    block_diag = rb == cb_

    units = []
    for step_idx in range(nc):
        for d, (x_ref, dt_ref, ac_ref, ar_ref, _) in enumerate(dirs):
            cc = step_idx if d == 0 else nc - 1 - step_idx
            rows = slice(cc * CHUNK, (cc + 1) * CHUNK)
            base = SSD_DT_LANE + d * SSD_HEADS
            acx = _expand_heads(ac_ref[0, rows, :], base)
            xr = x_ref[0, rows, :SSD_INNER].astype(F32) * _expand_heads(dt_ref[0, rows, :], base)
            aclast = acx[CHUNK - 1:CHUNK] if d == 0 else acx[0:1]
            acrow = ar_ref[0, cc]
            for g in range(SSD_GROUPS):
                sl = slice(g * SSD_GROUP_W, (g + 1) * SSD_GROUP_W)
                b0 = SSD_INNER + g * SSD_STATE
                c0 = SSD_INNER + (SSD_GROUPS + g) * SSD_STATE
                units.append(dict(d=d, g=g, rows=rows, sl=sl, bg=x_ref[0, rows, b0:b0 + SSD_STATE],
                                  cg=x_ref[0, rows, c0:c0 + SSD_STATE], acx=acx[:, sl], acrow=acrow[:, sl],
                                  aclast=aclast[:, sl], xr=xr[:, sl]))
    for t in units:
        t["cbt"] = _dot_nt(t["cg"], jnp.concatenate([t["bg"]] * SSD_HEADS_PER_GROUP, axis=0))
    for t in units:
        incl = (ii >= jj) if t["d"] == 0 else (ii <= jj)
        seg = jnp.where(incl, jnp.exp(jnp.minimum(t["acx"] - t["acrow"], 0.0)), 0.0)
        m = (t.pop("cbt") * seg).astype(BF16)
        xr = t.pop("xr")
        xr_bd = jnp.where(block_diag, jnp.concatenate([xr] * SSD_HEADS_PER_GROUP, axis=0), 0.0).astype(BF16)
        t["ydiag"] = _dot(m, xr_bd)
        t["st"] = _dot_tn(t["bg"], (xr * jnp.exp(t["aclast"] - t["acx"])).astype(BF16))

    state = {(d, g): h_ref[d, g] for d in range(2) for g in range(SSD_GROUPS)}
    per_step = 2 * SSD_GROUPS
    for step_idx in range(nc):
        for t in units[step_idx * per_step:(step_idx + 1) * per_step]:
            key = (t["d"], t["g"])
            y = t["ydiag"] + _dot(t["cg"], state[key].astype(BF16)) * jnp.exp(t["acx"])
            dirs[t["d"]][4][0, t["rows"], t["sl"]] = y
            state[key] = state[key] * jnp.exp(t["aclast"]) + t["st"]
    for (d, g), hval in state.items():
        h_ref[d, g] = hval


def _ssd_scan(xbc, dt, ac, acrow_f, acrow_b):
    B, L, _ = xbc.shape
    tr = TR_SEQ
    nc = tr // CHUNK
    nt = L // tr
    fwd = lambda w: pl.BlockSpec((1, tr, w), lambda b, i: (b, i, 0))
    bwd = lambda w: pl.BlockSpec((1, tr, w), lambda b, i: (b, nt - 1 - i, 0))
    arf = pl.BlockSpec((1, nc, 1, SSD_INNER), lambda b, i: (b, i, 0, 0))
    arb = pl.BlockSpec((1, nc, 1, SSD_INNER), lambda b, i: (b, nt - 1 - i, 0, 0))
    return pl.pallas_call(
        functools.partial(_ssd_scan_kernel, nc=nc),
        grid=(B, nt),
        in_specs=[fwd(SSD_XBC), fwd(LANE), fwd(LANE), arf, bwd(SSD_XBC), bwd(LANE), bwd(LANE), arb],
        out_specs=[fwd(SSD_INNER), bwd(SSD_INNER)],
        out_shape=[jax.ShapeDtypeStruct((B, L, SSD_INNER), F32)] * 2,
        scratch_shapes=[pltpu.VMEM((2, SSD_GROUPS, SSD_STATE, SSD_GROUP_W), F32)],
        compiler_params=_cparams("arbitrary", "arbitrary"),
        name="ssd_scan",
    )(xbc, dt, ac, acrow_f, xbc, dt, ac, acrow_b)


def _rope_pair(pair, cs):
    prod = pair * cs
    lane = lax.broadcasted_iota(jnp.int32, prod.shape, 1)
    return jnp.where(lane < MLA_ROPE, prod + pltpu.roll(prod, MLA_ROPE, 1), 0.0)


def _mla_attn_kernel(q_ref, k_ref, v_ref, o_ref):
    hs, tq = q_ref.shape[1], q_ref.shape[2]
    nk = k_ref.shape[2] // TK_ATTN
    steps = [(h, j) for h in range(hs) for j in range(nk)]

    def scores(h, j):
        return _dot_nt(q_ref[0, h], k_ref[0, h, j * TK_ATTN:(j + 1) * TK_ATTN, :])

    s_next = scores(*steps[0])
    for idx, (h, j) in enumerate(steps):
        s = s_next
        if idx + 1 < len(steps):
            s_next = scores(*steps[idx + 1])
        if j == 0:
            m = jnp.full((tq, 1), -jnp.inf, F32)
            acc = jnp.zeros((tq, MLA_V_PAD), F32)
        m_new = jnp.maximum(m, jnp.max(s, axis=-1, keepdims=True))
        p = jnp.exp2(s - m_new).astype(BF16)
        acc = jnp.exp2(m - m_new) * acc + _dot(p, v_ref[0, h, j * TK_ATTN:(j + 1) * TK_ATTN, :])
        m = m_new
        if j == nk - 1:
            o_ref[0, :, h * MLA_V:(h + 1) * MLA_V] = (acc[:, :MLA_V] / acc[:, MLA_V:]).astype(BF16)


def _mla_attn(q, k, v):
    B, H, L, _ = q.shape
    tq = TQ_ATTN
    kv_bytes_per_head = 2 * L * (MLA_QK_PAD + MLA_V_PAD) * 2
    hs = max(1, min(H, ATTN_KV_VMEM_BUDGET // kv_bytes_per_head))
    while H % hs:
        hs -= 1
    return pl.pallas_call(
        _mla_attn_kernel,
        grid=(B, H // hs, L // tq),
        in_specs=[pl.BlockSpec((1, hs, tq, MLA_QK_PAD), lambda b, h, i: (b, h, i, 0)),
                  pl.BlockSpec((1, hs, L, MLA_QK_PAD), lambda b, h, i: (b, h, 0, 0)),
                  pl.BlockSpec((1, hs, L, MLA_V_PAD), lambda b, h, i: (b, h, 0, 0))],
        out_specs=pl.BlockSpec((1, tq, hs * MLA_V), lambda b, h, i: (b, i, h)),
        out_shape=jax.ShapeDtypeStruct((B, L, H * MLA_V), BF16),
        compiler_params=_cparams("parallel", "parallel", "parallel"),
        name="mla_attn",
    )(q, k, v)


def _merge_kernel(h_ref, of_ref, ob_ref, za_ref, yf_ref, yb_ref, xbc_ref, zb_ref, oc_ref,
                  mixg_ref, gng_ref, dskip_ref, sng_ref, wa_ref, wb_ref, wc_ref, wg_ref, bg_ref, wo_ref,
                  out_ref):
    h = h_ref[...]
    u = _rms(h, mixg_ref[...]).astype(BF16)
    gates = jax.nn.sigmoid(_dot(u, wg_ref[...]) + bg_ref[...])

    o = of_ref[...] + ob_ref[...]
    za = za_ref[...].astype(F32)
    parts = []
    for hd in range(GDN_HEADS):
        sl = slice(hd * GDN_DV, (hd + 1) * GDN_DV)
        parts.append(_rms(o[:, sl], gng_ref[...]) * _silu(za[:, sl]))
    ya = _dot(jnp.concatenate(parts, axis=1).astype(BF16), wa_ref[...])

    y = yf_ref[...] + yb_ref[...] + xbc_ref[...].astype(F32) * dskip_ref[...]
    y = _rms(y * _silu(zb_ref[...].astype(F32)), sng_ref[...])
    yb = _dot(y.astype(BF16), wb_ref[...])

    yc = _dot(oc_ref[...], wc_ref[...])

    merged = (gates[:, :D_MODEL] * ya + gates[:, D_MODEL:2 * D_MODEL] * yb
              + gates[:, 2 * D_MODEL:] * yc)
    out_ref[...] = h + _dot(merged.astype(BF16), wo_ref[...])


def _merge(h2d, o_f, o_b, za, y_f, y_b, xbc_act, zb, oc, mix_gain, gdn_gain, dskip, ssd_gain,
           wa, wb, wc, wg, bg, wo):
    T = h2d.shape[0]
    tm = TM_MERGE
    row = lambda w: pl.BlockSpec((tm, w), lambda i: (i, 0))
    return pl.pallas_call(
        _merge_kernel,
        grid=(T // tm,),
        in_specs=[row(D_MODEL), row(GDN_V_W), row(GDN_V_W), row(GDN_V_W), row(SSD_INNER), row(SSD_INNER),
                  row(SSD_INNER), row(SSD_INNER), row(MLA_HEADS * MLA_V),
                  _const_spec((1, D_MODEL)), _const_spec((1, GDN_DV)), _const_spec((1, SSD_INNER)),
                  _const_spec((1, SSD_INNER)), _const_spec((GDN_V_W, D_MODEL)),
                  _const_spec((SSD_INNER, D_MODEL)), _const_spec((MLA_HEADS * MLA_V, D_MODEL)),
                  _const_spec((D_MODEL, N_BRANCH * D_MODEL)), _const_spec((1, N_BRANCH * D_MODEL)),
                  _const_spec((D_MODEL, D_MODEL))],
        out_specs=row(D_MODEL),
        out_shape=jax.ShapeDtypeStruct((T, D_MODEL), F32),
        compiler_params=_cparams("parallel"),
        name="merge",
    )(h2d, o_f, o_b, za, y_f, y_b, xbc_act, zb, oc, mix_gain, gdn_gain, dskip, ssd_gain,
      wa, wb, wc, wg, bg, wo)


def _rotate_half_cols(w):
    w1, w2 = jnp.split(w, 2, axis=-1)
    return jnp.concatenate([-w2, w1], axis=-1)


def _lane_row(values, base):
    row = jnp.zeros((1, LANE), F32)
    return row.at[0, base:base + values.shape[0]].set(values.astype(F32))


def _prep_layer(p, i):
    w_in = p["w_in"][i]
    c = 0
    seg = {}
    for name, width in (("qkv", GDN_QKV_W), ("za", GDN_V_W), ("ab", 4 * GDN_HEADS), ("zb", SSD_INNER),
                        ("xbc", SSD_XBC), ("dt", 2 * SSD_HEADS), ("cq", MLA_Q_LORA), ("ckv", MLA_KV_LORA),
                        ("kr", MLA_ROPE)):
        seg[name] = w_in[:, c:c + width]
        c += width
    w_conv = jnp.concatenate([seg["qkv"], seg["xbc"]], axis=1)
    w_rest = jnp.concatenate([seg["za"], seg["zb"], seg["cq"], seg["ckv"]], axis=1)
    pad = jnp.zeros((D_MODEL, LANE - 4 * GDN_HEADS - 2 * SSD_HEADS), F32)
    w_small = jnp.concatenate([seg["ab"], seg["dt"], pad, seg["kr"], _rotate_half_cols(seg["kr"])], axis=1)

    wq = p["mla_w_uq"][i].reshape(MLA_Q_LORA, MLA_HEADS, MLA_NOPE + MLA_ROPE)
    wq_rope = wq[..., MLA_NOPE:]
    wq = jnp.concatenate([wq, _rotate_half_cols(wq_rope)], axis=-1).reshape(MLA_Q_LORA, MLA_HEADS * MLA_QK_PAD)

    row = lambda v: v.reshape(1, -1).astype(F32)
    conv_pad = lambda w: jnp.concatenate([w, jnp.zeros((SUBLANE - CONV_W, w.shape[1]), F32)], axis=0)
    return dict(
        ffn1_norm=row(p["ffn1_norm"][i]), w_ffn1_in=p["w_ffn1_in"][i].astype(BF16),
        w_ffn1_out=p["w_ffn1_out"][i].astype(BF16),
        mix_norm=row(p["mix_norm"][i]), w_conv=w_conv.astype(BF16), w_rest=w_rest.astype(BF16),
        w_small=w_small.astype(BF16),
        gdn_conv=conv_pad(p["gdn_conv"][i]),
        gdn_alog=_lane_row(p["gdn_A_log"][i].reshape(-1), GDN_AB_LANE),
        gdn_bias=_lane_row(p["gdn_dt_bias"][i].reshape(-1), GDN_AB_LANE),
        gdn_norm=row(p["gdn_norm"][i]),
        ssd_conv=conv_pad(p["ssd_conv"][i]), ssd_conv_b=row(p["ssd_conv_b"][i]),
        ssd_alog=_lane_row(p["ssd_A_log"][i].reshape(-1), SSD_DT_LANE),
        ssd_bias=_lane_row(p["ssd_dt_bias"][i].reshape(-1), SSD_DT_LANE),
        ssd_dskip=row(jnp.repeat(p["ssd_D"][i], SSD_HEAD_DIM)), ssd_norm=row(p["ssd_norm"][i]),
        mla_q_norm=row(p["mla_q_norm"][i]), mla_kv_norm=row(p["mla_kv_norm"][i]),
        mla_wq=wq.astype(BF16), mla_wkv=p["mla_w_ukv"][i].astype(BF16),
        w_branch_a=p["w_branch_a"][i].astype(BF16), w_branch_b=p["w_branch_b"][i].astype(BF16),
        w_branch_c=p["w_branch_c"][i].astype(BF16), w_gate=p["w_gate"][i].astype(BF16),
        b_gate=row(p["b_gate"][i]), w_out=p["w_out"][i].astype(BF16),
        ffn2_norm=row(p["ffn2_norm"][i]), w_ffn2_in=p["w_ffn2_in"][i].astype(BF16),
        w_ffn2_out=p["w_ffn2_out"][i].astype(BF16),
    )


def _rope_table(L):
    inv_freq = jnp.power(ROPE_BASE, -jnp.arange(0, MLA_ROPE, 2, dtype=F32) / MLA_ROPE)
    ang = jnp.arange(L, dtype=F32)[:, None] * inv_freq[None, :]
    ang = jnp.concatenate([ang, ang], axis=-1)
    return jnp.concatenate([jnp.cos(ang), jnp.sin(ang)], axis=-1)


def _rows_to_chunk_rows(cols, B, L):
    n = cols.shape[-1]
    return cols.reshape(B, L // CHUNK, CHUNK, n).transpose(0, 1, 3, 2)


def _layer(x2d, B, L, cs, lp, final_gain, final):
    T = B * L
    h = _ffn(x2d, lp["ffn1_norm"], lp["w_ffn1_in"], lp["w_ffn1_out"], final_gain, False)
    q, k, v, gb, za, zb, xbc_act, dt, ac, qc, kc, vc = _mix(h.reshape(B, L, D_MODEL), cs, lp)

    grow = _rows_to_chunk_rows(gb[..., :GDN_CHAINS], B, L)
    o_f, o_b = _gdn_scan(q, k, v, gb, grow)

    acrow = _rows_to_chunk_rows(ac[..., SSD_DT_LANE:SSD_DT_LANE + 2 * SSD_HEADS], B, L)
    acrow = acrow.reshape(B, L // CHUNK, 2, 1, SSD_HEADS, 1, CHUNK)
    acrow = jnp.broadcast_to(acrow, (B, L // CHUNK, 2, 1, SSD_HEADS, SSD_HEAD_DIM // CHUNK, CHUNK))
    acrow = acrow.reshape(B, L // CHUNK, 2, 1, SSD_INNER)
    y_f, y_b = _ssd_scan(xbc_act, dt, ac, acrow[:, :, 0], acrow[:, :, 1])

    oc = _mla_attn(qc, kc, vc)

    flat = lambda a: a.reshape(T, a.shape[-1])
    h = _merge(h, flat(o_f), flat(o_b), flat(za), flat(y_f), flat(y_b), flat(xbc_act), flat(zb), flat(oc),
               lp["mix_norm"], lp["gdn_norm"], lp["ssd_dskip"], lp["ssd_norm"],
               lp["w_branch_a"], lp["w_branch_b"], lp["w_branch_c"], lp["w_gate"], lp["b_gate"], lp["w_out"])
    return _ffn(h, lp["ffn2_norm"], lp["w_ffn2_in"], lp["w_ffn2_out"], final_gain, final)


def _trunk(x, layers, final_gain):
    B, L, _ = x.shape
    cs = _rope_table(L)
    h = x.reshape(B * L, D_MODEL)
    for i, lp in enumerate(layers):
        h = _layer(h, B, L, cs, lp, final_gain, i == len(layers) - 1)
    return h.reshape(B, L, D_MODEL)


def kernel(x_prompt, x_sample, ffn1_norm, w_ffn1_in, w_ffn1_out, mix_norm, w_in, gdn_conv, gdn_A_log, gdn_dt_bias, gdn_norm, ssd_conv, ssd_conv_b, ssd_A_log, ssd_dt_bias, ssd_D, ssd_norm, mla_q_norm, mla_w_uq, mla_kv_norm, mla_w_ukv, w_branch_a, w_branch_b, w_branch_c, w_gate, b_gate, w_out, ffn2_norm, w_ffn2_in, w_ffn2_out, final_norm):
    p = dict(ffn1_norm=ffn1_norm, w_ffn1_in=w_ffn1_in, w_ffn1_out=w_ffn1_out, mix_norm=mix_norm, w_in=w_in,
             gdn_conv=gdn_conv, gdn_A_log=gdn_A_log, gdn_dt_bias=gdn_dt_bias, gdn_norm=gdn_norm,
             ssd_conv=ssd_conv, ssd_conv_b=ssd_conv_b, ssd_A_log=ssd_A_log, ssd_dt_bias=ssd_dt_bias,
             ssd_D=ssd_D, ssd_norm=ssd_norm, mla_q_norm=mla_q_norm, mla_w_uq=mla_w_uq,
             mla_kv_norm=mla_kv_norm, mla_w_ukv=mla_w_ukv, w_branch_a=w_branch_a, w_branch_b=w_branch_b,
             w_branch_c=w_branch_c, w_gate=w_gate, b_gate=b_gate, w_out=w_out, ffn2_norm=ffn2_norm,
             w_ffn2_in=w_ffn2_in, w_ffn2_out=w_ffn2_out)
    layers = [_prep_layer(p, i) for i in range(DEPTH)]
    final_gain = final_norm.reshape(1, D_MODEL).astype(F32)
    return (_trunk(x_prompt, layers, final_gain), _trunk(x_sample, layers, final_gain))
```

```python
import functools

import jax
import jax.numpy as jnp
from jax import lax
from jax.experimental import pallas as pl
from jax.experimental.pallas import tpu as pltpu

F32 = jnp.float32
BF16 = jnp.bfloat16

D_MODEL = 1024
DEPTH = 2
EPS = 1e-6
CONV_W = 5
CONV_HALF = CONV_W // 2
CHUNK = 64

GDN_HEADS = 4
GDN_DK = 128
GDN_DV = 128
GDN_QK_W = GDN_HEADS * GDN_DK
GDN_V_W = GDN_HEADS * GDN_DV
GDN_QKV_W = 2 * GDN_QK_W + GDN_V_W
GDN_CHAINS = 2 * GDN_HEADS

SSD_HEADS = 8
SSD_HEAD_DIM = 64
SSD_INNER = SSD_HEADS * SSD_HEAD_DIM
SSD_GROUPS = 2
SSD_STATE = 128
SSD_XBC = SSD_INNER + 2 * SSD_GROUPS * SSD_STATE
SSD_GROUP_W = SSD_INNER // SSD_GROUPS
SSD_HEADS_PER_GROUP = SSD_HEADS // SSD_GROUPS
HEAD_SHIFT = 6
assert (1 << HEAD_SHIFT) == SSD_HEAD_DIM == CHUNK

MLA_HEADS = 4
MLA_Q_LORA = 256
MLA_KV_LORA = 256
MLA_NOPE = 128
MLA_ROPE = 64
MLA_V = 128
MLA_QK_PAD = 256
MLA_SCALE = (MLA_NOPE + MLA_ROPE) ** -0.5
MLA_V_PAD = 256
LOG2_E = 1.4426950408889634
ROPE_BASE = 10000.0

FFN_HIDDEN = 2816
N_BRANCH = 3

CONV_GROUP_W = 512
SMALL_W = 256
LANE = 128
SUBLANE = 8
GDN_AB_LANE = 0
SSD_DT_LANE = 16

VMEM_LIMIT = 56 * 1024 * 1024

TM_FFN = 1024
FFN_CHUNK_STARTS = (0, 768, 1536, 2304)
TM_MIX = 512
TM_MERGE = 512
TR_SEQ = 512
TQ_ATTN = 512
TK_ATTN = 512
ATTN_KV_VMEM_BUDGET = 32 * 1024 * 1024


def _cparams(*sem):
    return pltpu.CompilerParams(dimension_semantics=sem, vmem_limit_bytes=VMEM_LIMIT)


def _const_spec(shape):
    nd = len(shape)
    return pl.BlockSpec(shape, lambda *_: (0,) * nd, pipeline_mode=pl.Buffered(1))


def _rms(x, gain):
    return x * lax.rsqrt(jnp.mean(x * x, axis=-1, keepdims=True) + EPS) * gain


def _silu(x):
    t = 0.5 * x
    return t + t * jnp.tanh(t)


def _softplus(x):
    return jnp.maximum(x, 0.0) + jnp.log1p(jnp.exp(-jnp.abs(x)))


def _dot(a, b):
    return jnp.dot(a, b, preferred_element_type=F32)


def _dot_nt(a, b):
    return lax.dot_general(a, b, (((1,), (1,)), ((), ())), preferred_element_type=F32)


def _dot_tn(a, b):
    return lax.dot_general(a, b, (((0,), (0,)), ((), ())), preferred_element_type=F32)


def _ffn_kernel(x_ref, g_ref, win_ref, wout_ref, fin_ref, o_ref, *, final):
    x = x_ref[...]
    xn = _rms(x, g_ref[...]).astype(BF16)

    def up(c0, c1):
        return _dot(xn, win_ref[:, c0:c1]), _dot(xn, win_ref[:, FFN_HIDDEN + c0:FFN_HIDDEN + c1])

    bounds = list(zip(FFN_CHUNK_STARTS, FFN_CHUNK_STARTS[1:] + (FFN_HIDDEN,)))
    nxt = up(*bounds[0])
    y = None
    for idx, (c0, c1) in enumerate(bounds):
        gate, val = nxt
        if idx + 1 < len(bounds):
            nxt = up(*bounds[idx + 1])
        part = _dot((_silu(gate) * val).astype(BF16), wout_ref[c0:c1, :])
        y = part if y is None else y + part
    h = x + 0.5 * y
    if final:
        h = _rms(h, fin_ref[...])
    o_ref[...] = h


def _ffn(x2d, gain, w_in, w_out, fin, final):
    T = x2d.shape[0]
    tm = TM_FFN
    row = pl.BlockSpec((tm, D_MODEL), lambda i: (i, 0))
    return pl.pallas_call(
        functools.partial(_ffn_kernel, final=final),
        grid=(T // tm,),
        in_specs=[row, _const_spec((1, D_MODEL)), _const_spec((D_MODEL, 2 * FFN_HIDDEN)),
                  _const_spec((FFN_HIDDEN, D_MODEL)), _const_spec((1, D_MODEL))],
        out_specs=row,
        out_shape=jax.ShapeDtypeStruct((T, D_MODEL), F32),
        compiler_params=_cparams("parallel"),
        name="ffn",
    )(x2d, gain, w_in, w_out, fin)


def _mix_kernel(cur_ref, prev_ref, next_ref, cs_ref, g_ref, wconv_ref, wrest_ref, wsmall_ref,
                gconv_ref, galog_ref, gbias_ref, sconv_ref, sconvb_ref, salog_ref, sbias_ref,
                qn_ref, kvn_ref, wq_ref, wkv_ref,
                q_ref, k_ref, v_ref, gb_ref, za_ref, zb_ref, xbc_ref, dt_ref, ac_ref,
                mq_ref, mk_ref, mv_ref):
    i = pl.program_id(1)
    tm = cur_ref.shape[1]
    body = slice(SUBLANE, SUBLANE + tm)
    first = i == 0
    last = i == pl.num_programs(1) - 1
    hx = jnp.concatenate([prev_ref[0], cur_ref[0], next_ref[0]], axis=0)
    u = _rms(hx, g_ref[...]).astype(BF16)

    def conv_cols(c0, w_ref, wc0):
        cols = _dot(u, wconv_ref[:, c0:c0 + CONV_GROUP_W])
        xe = jnp.concatenate([jnp.where(first, 0.0, cols[:SUBLANE]), cols[body],
                              jnp.where(last, 0.0, cols[SUBLANE + tm:])], axis=0)
        return _conv(xe, w_ref, wc0, CONV_GROUP_W)

    for c0, o_ref, scale in ((0, q_ref, GDN_DK ** -0.5), (GDN_QK_W, k_ref, 1.0)):
        a = _silu(conv_cols(c0, gconv_ref, c0))
        for h in range(GDN_HEADS):
            sl = slice(h * GDN_DK, (h + 1) * GDN_DK)
            seg = a[:, sl]
            o_ref[0, :, sl] = (seg * (lax.rsqrt(jnp.sum(seg * seg, axis=-1, keepdims=True) + EPS)
                                      * scale)).astype(BF16)
    v_ref[0] = _silu(conv_cols(2 * GDN_QK_W, gconv_ref, 2 * GDN_QK_W)).astype(BF16)

    for c0 in range(0, SSD_XBC, CONV_GROUP_W):
        xbc_ref[0, :, c0:c0 + CONV_GROUP_W] = _silu(
            conv_cols(GDN_QKV_W + c0, sconv_ref, c0) + sconvb_ref[:, c0:c0 + CONV_GROUP_W]).astype(BF16)

    rest = _dot(u, wrest_ref[...])[body]
    za_ref[0] = rest[:, :GDN_V_W].astype(BF16)
    zb_ref[0] = rest[:, GDN_V_W:GDN_V_W + SSD_INNER].astype(BF16)
    small = _dot(u, wsmall_ref[...])[body]

    s = small[:, :LANE]
    lane = lax.broadcasted_iota(jnp.int32, s.shape, 1)
    is_gdn = lane < GDN_CHAINS
    is_ssd = (lane >= SSD_DT_LANE) & (lane < SSD_DT_LANE + 2 * SSD_HEADS)
    sp = _softplus(s + gbias_ref[...] + sbias_ref[...])
    log_decay = jnp.where(is_gdn | is_ssd, -jnp.exp(galog_ref[...] + salog_ref[...]) * sp, 0.0)
    fwd = (lane < GDN_HEADS) | ((lane >= SSD_DT_LANE) & (lane < SSD_DT_LANE + SSD_HEADS))
    csum = _chunk_cumsum(log_decay, fwd)
    gb_ref[0] = jnp.where(is_gdn, csum, jax.nn.sigmoid(s))
    dt_ref[0] = jnp.where(is_ssd, sp, 0.0)
    ac_ref[0] = jnp.where(is_ssd, csum, 0.0)

    c = rest[:, GDN_V_W + SSD_INNER:]
    cs = cs_ref[...]
    cq = _rms(c[:, :MLA_Q_LORA], qn_ref[...]).astype(BF16)
    ckv = _rms(c[:, MLA_Q_LORA:], kvn_ref[...]).astype(BF16)
    qa = _dot(cq, wq_ref[...]) * (MLA_SCALE * LOG2_E)
    kva = _dot(ckv, wkv_ref[...])
    k_rope = _rope_pair(small[:, LANE:], cs).astype(BF16)
    for h in range(MLA_HEADS):
        base = h * MLA_QK_PAD
        mq_ref[0, h, :, :MLA_NOPE] = qa[:, base:base + MLA_NOPE].astype(BF16)
        mq_ref[0, h, :, MLA_NOPE:] = _rope_pair(qa[:, base + MLA_NOPE:base + MLA_QK_PAD], cs).astype(BF16)
        mk_ref[0, h, :, :MLA_NOPE] = kva[:, base:base + MLA_NOPE].astype(BF16)
        mk_ref[0, h, :, MLA_NOPE:] = k_rope
        mv_ref[0, h, :, :MLA_V] = kva[:, base + MLA_NOPE:base + MLA_NOPE + MLA_V].astype(BF16)
        mv_ref[0, h, :, MLA_V:] = jnp.ones((tm, MLA_V_PAD - MLA_V), BF16)


def _mix(h3, cs, lp):
    B, L, _ = h3.shape
    tm = TM_MIX
    cur, prev, nxt = _seq_specs(tm, D_MODEL, L)
    out = lambda w: pl.BlockSpec((1, tm, w), lambda b, i: (b, i, 0))
    head_out = lambda w: pl.BlockSpec((1, MLA_HEADS, tm, w), lambda b, i: (b, 0, i, 0))
    seq_shape = lambda w, dt: jax.ShapeDtypeStruct((B, L, w), dt)
    head_shape = lambda w: jax.ShapeDtypeStruct((B, MLA_HEADS, L, w), BF16)
    consts = (lp["mix_norm"], lp["w_conv"], lp["w_rest"], lp["w_small"],
              lp["gdn_conv"], lp["gdn_alog"], lp["gdn_bias"],
              lp["ssd_conv"], lp["ssd_conv_b"], lp["ssd_alog"], lp["ssd_bias"],
              lp["mla_q_norm"], lp["mla_kv_norm"], lp["mla_wq"], lp["mla_wkv"])
    return pl.pallas_call(
        _mix_kernel,
        grid=(B, L // tm),
        in_specs=[cur, prev, nxt, pl.BlockSpec((tm, LANE), lambda b, i: (i, 0))]
        + [_const_spec(a.shape) for a in consts],
        out_specs=[out(GDN_QK_W), out(GDN_QK_W), out(GDN_V_W), out(LANE), out(GDN_V_W), out(SSD_INNER),
                   out(SSD_XBC), out(LANE), out(LANE),
                   head_out(MLA_QK_PAD), head_out(MLA_QK_PAD), head_out(MLA_V_PAD)],
        out_shape=[seq_shape(GDN_QK_W, BF16), seq_shape(GDN_QK_W, BF16), seq_shape(GDN_V_W, BF16),
                   seq_shape(LANE, F32), seq_shape(GDN_V_W, BF16), seq_shape(SSD_INNER, BF16),
                   seq_shape(SSD_XBC, BF16), seq_shape(LANE, F32), seq_shape(LANE, F32),
                   head_shape(MLA_QK_PAD), head_shape(MLA_QK_PAD), head_shape(MLA_V_PAD)],
        compiler_params=_cparams("parallel", "parallel"),
        name="mix",
    )(h3, h3, h3, cs, *consts)


def _seq_specs(tr, width, L):
    per = tr // SUBLANE
    last = L // SUBLANE - 1
    cur = pl.BlockSpec((1, tr, width), lambda b, i: (b, i, 0))
    prev = pl.BlockSpec((1, SUBLANE, width), lambda b, i: (b, jnp.maximum(i * per - 1, 0), 0))
    nxt = pl.BlockSpec((1, SUBLANE, width), lambda b, i: (b, jnp.minimum((i + 1) * per, last), 0))
    return cur, prev, nxt


def _conv(xe, w_ref, c0, width):
    n = xe.shape[0]
    groups = (n - 2 * SUBLANE) // SUBLANE
    x3 = xe.reshape(n // SUBLANE, SUBLANE, width)
    sub = lax.broadcasted_iota(jnp.int32, (groups, SUBLANE, width), 1)
    acc = None
    for k in range(CONV_W):
        s = CONV_HALF - k
        if s == 0:
            shifted = x3[1:1 + groups]
        else:
            r = pltpu.roll(x3, s % SUBLANE, 1)
            if s > 0:
                shifted = jnp.where(sub < s, r[0:groups], r[1:1 + groups])
            else:
                shifted = jnp.where(sub >= SUBLANE + s, r[2:2 + groups], r[1:1 + groups])
        term = shifted * w_ref[k:k + 1, c0:c0 + width]
        acc = term if acc is None else acc + term
    return acc.reshape(groups * SUBLANE, width)


def _chunk_cumsum(g, lane_is_fwd):
    tr = g.shape[0]
    pos = lax.broadcasted_iota(jnp.int32, g.shape, 0) & (CHUNK - 1)
    pre = g
    suf = g
    s = 1
    while s < CHUNK:
        pre = pre + jnp.where(pos >= s, pltpu.roll(pre, s, 0), 0.0)
        suf = suf + jnp.where(pos < CHUNK - s, pltpu.roll(suf, tr - s, 0), 0.0)
        s *= 2
    return jnp.where(lane_is_fwd, pre, suf)


NEUMANN_STEPS = 6
assert 2 ** NEUMANN_STEPS == CHUNK


def _gdn_scan_kernel(qf_ref, kf_ref, vf_ref, gbf_ref, grf_ref, qb_ref, kb_ref, vb_ref, gbb_ref, grb_ref,
                     of_ref, ob_ref, s_ref, *, nc):
    @pl.when(pl.program_id(1) == 0)
    def _():
        s_ref[...] = jnp.zeros_like(s_ref)

    dirs = ((qf_ref, kf_ref, vf_ref, gbf_ref, grf_ref, of_ref),
            (qb_ref, kb_ref, vb_ref, gbb_ref, grb_ref, ob_ref))
    ii = lax.broadcasted_iota(jnp.int32, (CHUNK, CHUNK), 0)
    jj = lax.broadcasted_iota(jnp.int32, (CHUNK, CHUNK), 1)
    eye = jnp.where(ii == jj, 1.0, 0.0)
    t_half = lax.broadcasted_iota(jnp.int32, (CHUNK, 2 * CHUNK), 1) >= CHUNK

    units = []
    for step_idx in range(nc):
        for d, (q_ref, k_ref, v_ref, gb_ref, gr_ref, _) in enumerate(dirs):
            cc = step_idx if d == 0 else nc - 1 - step_idx
            rows = slice(cc * CHUNK, (cc + 1) * CHUNK)
            gb = gb_ref[0, rows, :]
            grow_all = gr_ref[0, cc]
            for h in range(GDN_HEADS):
                j = d * GDN_HEADS + h
                sl = slice(h * GDN_DK, (h + 1) * GDN_DK)
                units.append(dict(d=d, j=j, rows=rows, sl=sl, q=q_ref[0, rows, sl], k=k_ref[0, rows, sl],
                                  v=v_ref[0, rows, sl], gcol=gb[:, j:j + 1], grow=grow_all[j:j + 1, :],
                                  bcol=gb[:, GDN_CHAINS + j:GDN_CHAINS + j + 1]))
    for t in units:
        r = _dot_nt(jnp.concatenate([t["k"], t["q"]], axis=0), t["k"])
        t["kk"] = r[:CHUNK]
        t["qk"] = r[CHUNK:]
    for t in units:
        gcol, bcol = t["gcol"], t["bcol"]
        incl = (ii >= jj) if t["d"] == 0 else (ii <= jj)
        dec = jnp.where(incl, jnp.exp(jnp.minimum(gcol - t["grow"], 0.0)), 0.0)
        nmat = -(bcol * t.pop("kk")) * jnp.where(ii == jj, 0.0, dec)
        t["nt"] = jnp.concatenate([nmat, eye], axis=1)
        t["attn"] = (t.pop("qk") * dec).astype(BF16)
        kf = t["k"].astype(F32)
        eg = jnp.exp(gcol)
        glast = gcol[CHUNK - 1:CHUNK] if t["d"] == 0 else gcol[0:1]
        t["rhs"] = (bcol * jnp.concatenate([t["v"].astype(F32), kf * eg], axis=1)).astype(BF16)
        t["qd"] = t["q"].astype(F32) * eg
        t["kd"] = (kf * jnp.exp(glast - gcol)).astype(BF16)
        t["eglast"] = jnp.exp(glast)
    for step in range(NEUMANN_STEPS):
        for t in units:
            ntb = t["nt"].astype(BF16)
            t["y"] = _dot(ntb[:, :CHUNK], ntb)
        for t in units:
            t["nt"] = t.pop("y") + jnp.where(t_half, t["nt"], 0.0)
    for t in units:
        x = _dot(t.pop("nt")[:, CHUNK:].astype(BF16), t.pop("rhs"))
        t["u"] = x[:, :GDN_DV]
        t["wq"] = jnp.concatenate([x[:, GDN_DV:], t.pop("qd")], axis=0).astype(BF16)

    state = [s_ref[j] for j in range(GDN_CHAINS)]
    for step_idx in range(nc):
        group = units[step_idx * GDN_CHAINS:(step_idx + 1) * GDN_CHAINS]
        for t in group:
            t["r"] = _dot(t["wq"], state[t["j"]].astype(BF16))
        for t in group:
            t["vnew"] = (t["u"] - t["r"][:CHUNK]).astype(BF16)
        for t in group:
            t["o"] = t["r"][CHUNK:] + _dot(t["attn"], t["vnew"])
            state[t["j"]] = state[t["j"]] * t["eglast"] + _dot_tn(t["kd"], t["vnew"])
        for t in group:
            dirs[t["d"]][5][0, t["rows"], t["sl"]] = t["o"]
    for j in range(GDN_CHAINS):
        s_ref[j] = state[j]


def _gdn_scan(q, k, v, gb, grow):
    B, L, _ = q.shape
    tr = TR_SEQ
    nc = tr // CHUNK
    nt = L // tr
    fwd = lambda w: pl.BlockSpec((1, tr, w), lambda b, i: (b, i, 0))
    bwd = lambda w: pl.BlockSpec((1, tr, w), lambda b, i: (b, nt - 1 - i, 0))
    grf = pl.BlockSpec((1, nc, GDN_CHAINS, CHUNK), lambda b, i: (b, i, 0, 0))
    grb = pl.BlockSpec((1, nc, GDN_CHAINS, CHUNK), lambda b, i: (b, nt - 1 - i, 0, 0))
    return pl.pallas_call(
        functools.partial(_gdn_scan_kernel, nc=nc),
        grid=(B, nt),
        in_specs=[fwd(GDN_QK_W), fwd(GDN_QK_W), fwd(GDN_V_W), fwd(LANE), grf,
                  bwd(GDN_QK_W), bwd(GDN_QK_W), bwd(GDN_V_W), bwd(LANE), grb],
        out_specs=[fwd(GDN_V_W), bwd(GDN_V_W)],
        out_shape=[jax.ShapeDtypeStruct((B, L, GDN_V_W), F32)] * 2,
        scratch_shapes=[pltpu.VMEM((GDN_CHAINS, GDN_DK, GDN_DV), F32)],
        compiler_params=_cparams("arbitrary", "arbitrary"),
        name="gdn_scan",
    )(q, k, v, gb, grow, q, k, v, gb, grow)


def _expand_heads(cols, base):
    rows = cols.shape[0]
    low_half = lax.broadcasted_iota(jnp.int32, (rows, LANE), 1) < SSD_HEAD_DIM
    tiles = []
    for h in range(0, SSD_HEADS, LANE // SSD_HEAD_DIM):
        tiles.append(jnp.where(low_half, cols[:, base + h:base + h + 1], cols[:, base + h + 1:base + h + 2]))
    return jnp.concatenate(tiles, axis=1)


def _ssd_scan_kernel(xf_ref, dtf_ref, acf_ref, arf_ref, xb_ref, dtb_ref, acb_ref, arb_ref,
                     yf_ref, yb_ref, h_ref, *, nc):
    @pl.when(pl.program_id(1) == 0)
    def _():
        h_ref[...] = jnp.zeros_like(h_ref)

    dirs = ((xf_ref, dtf_ref, acf_ref, arf_ref, yf_ref), (xb_ref, dtb_ref, acb_ref, arb_ref, yb_ref))
    ii = lax.broadcasted_iota(jnp.int32, (CHUNK, SSD_GROUP_W), 0)
    jj = lax.broadcasted_iota(jnp.int32, (CHUNK, SSD_GROUP_W), 1) & (CHUNK - 1)
    rb = lax.broadcasted_iota(jnp.int32, (SSD_GROUP_W, SSD_GROUP_W), 0) >> HEAD_SHIFT
    cb_ = lax.broadcasted_iota(jnp.int32, (SSD_GROUP_W, SSD_GROUP_W), 1) >> HEAD_SHIFT
    block_diag = rb == cb_

    units = []
    for step_idx in range(nc):
        for d, (x_ref, dt_ref, ac_ref, ar_ref, _) in enumerate(dirs):
            cc = step_idx if d == 0 else nc - 1 - step_idx
            rows = slice(cc * CHUNK, (cc + 1) * CHUNK)
            base = SSD_DT_LANE + d * SSD_HEADS
            acx = _expand_heads(ac_ref[0, rows, :], base)
            xr = x_ref[0, rows, :SSD_INNER].astype(F32) * _expand_heads(dt_ref[0, rows, :], base)
            aclast = acx[CHUNK - 1:CHUNK] if d == 0 else acx[0:1]
            acrow = ar_ref[0, cc]
            for g in range(SSD_GROUPS):
                sl = slice(g * SSD_GROUP_W, (g + 1) * SSD_GROUP_W)
                b0 = SSD_INNER + g * SSD_STATE
                c0 = SSD_INNER + (SSD_GROUPS + g) * SSD_STATE
                units.append(dict(d=d, g=g, rows=rows, sl=sl, bg=x_ref[0, rows, b0:b0 + SSD_STATE],
                                  cg=x_ref[0, rows, c0:c0 + SSD_STATE], acx=acx[:, sl], acrow=acrow[:, sl],
                                  aclast=aclast[:, sl], xr=xr[:, sl]))
    for t in units:
        t["cbt"] = _dot_nt(t["cg"], jnp.concatenate([t["bg"]] * SSD_HEADS_PER_GROUP, axis=0))
    for t in units:
        incl = (ii >= jj) if t["d"] == 0 else (ii <= jj)
        seg = jnp.where(incl, jnp.exp(jnp.minimum(t["acx"] - t["acrow"], 0.0)), 0.0)
        m = (t.pop("cbt") * seg).astype(BF16)
        xr = t.pop("xr")
        xr_bd = jnp.where(block_diag, jnp.concatenate([xr] * SSD_HEADS_PER_GROUP, axis=0), 0.0).astype(BF16)
        t["ydiag"] = _dot(m, xr_bd)
        t["st"] = _dot_tn(t["bg"], (xr * jnp.exp(t["aclast"] - t["acx"])).astype(BF16))

    state = {(d, g): h_ref[d, g] for d in range(2) for g in range(SSD_GROUPS)}
    per_step = 2 * SSD_GROUPS
    for step_idx in range(nc):
        for t in units[step_idx * per_step:(step_idx + 1) * per_step]:
            key = (t["d"], t["g"])
            y = t["ydiag"] + _dot(t["cg"], state[key].astype(BF16)) * jnp.exp(t["acx"])
            dirs[t["d"]][4][0, t["rows"], t["sl"]] = y
            state[key] = state[key] * jnp.exp(t["aclast"]) + t["st"]
    for (d, g), hval in state.items():
        h_ref[d, g] = hval


def _ssd_scan(xbc, dt, ac, acrow_f, acrow_b):
    B, L, _ = xbc.shape
    tr = TR_SEQ
    nc = tr // CHUNK
    nt = L // tr
    fwd = lambda w: pl.BlockSpec((1, tr, w), lambda b, i: (b, i, 0))
    bwd = lambda w: pl.BlockSpec((1, tr, w), lambda b, i: (b, nt - 1 - i, 0))
    arf = pl.BlockSpec((1, nc, 1, SSD_INNER), lambda b, i: (b, i, 0, 0))
    arb = pl.BlockSpec((1, nc, 1, SSD_INNER), lambda b, i: (b, nt - 1 - i, 0, 0))
    return pl.pallas_call(
        functools.partial(_ssd_scan_kernel, nc=nc),
        grid=(B, nt),
        in_specs=[fwd(SSD_XBC), fwd(LANE), fwd(LANE), arf, bwd(SSD_XBC), bwd(LANE), bwd(LANE), arb],
        out_specs=[fwd(SSD_INNER), bwd(SSD_INNER)],
        out_shape=[jax.ShapeDtypeStruct((B, L, SSD_INNER), F32)] * 2,
        scratch_shapes=[pltpu.VMEM((2, SSD_GROUPS, SSD_STATE, SSD_GROUP_W), F32)],
        compiler_params=_cparams("arbitrary", "arbitrary"),
        name="ssd_scan",
    )(xbc, dt, ac, acrow_f, xbc, dt, ac, acrow_b)


def _rope_pair(pair, cs):
    prod = pair * cs
    lane = lax.broadcasted_iota(jnp.int32, prod.shape, 1)
    return jnp.where(lane < MLA_ROPE, prod + pltpu.roll(prod, MLA_ROPE, 1), 0.0)


def _mla_attn_kernel(q_ref, k_ref, v_ref, o_ref):
    hs, tq = q_ref.shape[1], q_ref.shape[2]
    nk = k_ref.shape[2] // TK_ATTN
    steps = [(h, j) for h in range(hs) for j in range(nk)]

    def scores(h, j):
        return _dot_nt(q_ref[0, h], k_ref[0, h, j * TK_ATTN:(j + 1) * TK_ATTN, :])

    s_next = scores(*steps[0])
    for idx, (h, j) in enumerate(steps):
        s = s_next
        if idx + 1 < len(steps):
            s_next = scores(*steps[idx + 1])
        if j == 0:
            m = jnp.full((tq, 1), -jnp.inf, F32)
            acc = jnp.zeros((tq, MLA_V_PAD), F32)
        m_new = jnp.maximum(m, jnp.max(s, axis=-1, keepdims=True))
        p = jnp.exp2(s - m_new).astype(BF16)
        acc = jnp.exp2(m - m_new) * acc + _dot(p, v_ref[0, h, j * TK_ATTN:(j + 1) * TK_ATTN, :])
        m = m_new
        if j == nk - 1:
            o_ref[0, :, h * MLA_V:(h + 1) * MLA_V] = (acc[:, :MLA_V] / acc[:, MLA_V:]).astype(BF16)


def _mla_attn(q, k, v):
    B, H, L, _ = q.shape
    tq = TQ_ATTN
    kv_bytes_per_head = 2 * L * (MLA_QK_PAD + MLA_V_PAD) * 2
    hs = max(1, min(H, ATTN_KV_VMEM_BUDGET // kv_bytes_per_head))
    while H % hs:
        hs -= 1
    return pl.pallas_call(
        _mla_attn_kernel,
        grid=(B, H // hs, L // tq),
        in_specs=[pl.BlockSpec((1, hs, tq, MLA_QK_PAD), lambda b, h, i: (b, h, i, 0)),
                  pl.BlockSpec((1, hs, L, MLA_QK_PAD), lambda b, h, i: (b, h, 0, 0)),
                  pl.BlockSpec((1, hs, L, MLA_V_PAD), lambda b, h, i: (b, h, 0, 0))],
        out_specs=pl.BlockSpec((1, tq, hs * MLA_V), lambda b, h, i: (b, i, h)),
        out_shape=jax.ShapeDtypeStruct((B, L, H * MLA_V), BF16),
        compiler_params=_cparams("parallel", "parallel", "parallel"),
        name="mla_attn",
    )(q, k, v)


def _merge_kernel(h_ref, of_ref, ob_ref, za_ref, yf_ref, yb_ref, xbc_ref, zb_ref, oc_ref,
                  mixg_ref, gng_ref, dskip_ref, sng_ref, wa_ref, wb_ref, wc_ref, wg_ref, bg_ref, wo_ref,
                  out_ref):
    h = h_ref[...]
    u = _rms(h, mixg_ref[...]).astype(BF16)
    gates = jax.nn.sigmoid(_dot(u, wg_ref[...]) + bg_ref[...])

    o = of_ref[...] + ob_ref[...]
    za = za_ref[...].astype(F32)
    parts = []
    for hd in range(GDN_HEADS):
        sl = slice(hd * GDN_DV, (hd + 1) * GDN_DV)
        parts.append(_rms(o[:, sl], gng_ref[...]) * _silu(za[:, sl]))
    ya = _dot(jnp.concatenate(parts, axis=1).astype(BF16), wa_ref[...])

    y = yf_ref[...] + yb_ref[...] + xbc_ref[...].astype(F32) * dskip_ref[...]
    y = _rms(y * _silu(zb_ref[...].astype(F32)), sng_ref[...])
    yb = _dot(y.astype(BF16), wb_ref[...])

    yc = _dot(oc_ref[...], wc_ref[...])

    merged = (gates[:, :D_MODEL] * ya + gates[:, D_MODEL:2 * D_MODEL] * yb
              + gates[:, 2 * D_MODEL:] * yc)
    out_ref[...] = h + _dot(merged.astype(BF16), wo_ref[...])


def _merge(h2d, o_f, o_b, za, y_f, y_b, xbc_act, zb, oc, mix_gain, gdn_gain, dskip, ssd_gain,
           wa, wb, wc, wg, bg, wo):
    T = h2d.shape[0]
    tm = TM_MERGE
    row = lambda w: pl.BlockSpec((tm, w), lambda i: (i, 0))
    return pl.pallas_call(
        _merge_kernel,
        grid=(T // tm,),
        in_specs=[row(D_MODEL), row(GDN_V_W), row(GDN_V_W), row(GDN_V_W), row(SSD_INNER), row(SSD_INNER),
                  row(SSD_INNER), row(SSD_INNER), row(MLA_HEADS * MLA_V),
                  _const_spec((1, D_MODEL)), _const_spec((1, GDN_DV)), _const_spec((1, SSD_INNER)),
                  _const_spec((1, SSD_INNER)), _const_spec((GDN_V_W, D_MODEL)),
                  _const_spec((SSD_INNER, D_MODEL)), _const_spec((MLA_HEADS * MLA_V, D_MODEL)),
                  _const_spec((D_MODEL, N_BRANCH * D_MODEL)), _const_spec((1, N_BRANCH * D_MODEL)),
                  _const_spec((D_MODEL, D_MODEL))],
        out_specs=row(D_MODEL),
        out_shape=jax.ShapeDtypeStruct((T, D_MODEL), F32),
        compiler_params=_cparams("parallel"),
        name="merge",
    )(h2d, o_f, o_b, za, y_f, y_b, xbc_act, zb, oc, mix_gain, gdn_gain, dskip, ssd_gain,
      wa, wb, wc, wg, bg, wo)


def _rotate_half_cols(w):
    w1, w2 = jnp.split(w, 2, axis=-1)
    return jnp.concatenate([-w2, w1], axis=-1)


def _lane_row(values, base):
    row = jnp.zeros((1, LANE), F32)
    return row.at[0, base:base + values.shape[0]].set(values.astype(F32))


def _prep_layer(p, i):
    w_in = p["w_in"][i]
    c = 0
    seg = {}
    for name, width in (("qkv", GDN_QKV_W), ("za", GDN_V_W), ("ab", 4 * GDN_HEADS), ("zb", SSD_INNER),
                        ("xbc", SSD_XBC), ("dt", 2 * SSD_HEADS), ("cq", MLA_Q_LORA), ("ckv", MLA_KV_LORA),
                        ("kr", MLA_ROPE)):
        seg[name] = w_in[:, c:c + width]
        c += width
    w_conv = jnp.concatenate([seg["qkv"], seg["xbc"]], axis=1)
    w_rest = jnp.concatenate([seg["za"], seg["zb"], seg["cq"], seg["ckv"]], axis=1)
    pad = jnp.zeros((D_MODEL, LANE - 4 * GDN_HEADS - 2 * SSD_HEADS), F32)
    w_small = jnp.concatenate([seg["ab"], seg["dt"], pad, seg["kr"], _rotate_half_cols(seg["kr"])], axis=1)

    wq = p["mla_w_uq"][i].reshape(MLA_Q_LORA, MLA_HEADS, MLA_NOPE + MLA_ROPE)
    wq_rope = wq[..., MLA_NOPE:]
    wq = jnp.concatenate([wq, _rotate_half_cols(wq_rope)], axis=-1).reshape(MLA_Q_LORA, MLA_HEADS * MLA_QK_PAD)

    row = lambda v: v.reshape(1, -1).astype(F32)
    conv_pad = lambda w: jnp.concatenate([w, jnp.zeros((SUBLANE - CONV_W, w.shape[1]), F32)], axis=0)
    return dict(
        ffn1_norm=row(p["ffn1_norm"][i]), w_ffn1_in=p["w_ffn1_in"][i].astype(BF16),
        w_ffn1_out=p["w_ffn1_out"][i].astype(BF16),
        mix_norm=row(p["mix_norm"][i]), w_conv=w_conv.astype(BF16), w_rest=w_rest.astype(BF16),
        w_small=w_small.astype(BF16),
        gdn_conv=conv_pad(p["gdn_conv"][i]),
        gdn_alog=_lane_row(p["gdn_A_log"][i].reshape(-1), GDN_AB_LANE),
        gdn_bias=_lane_row(p["gdn_dt_bias"][i].reshape(-1), GDN_AB_LANE),
        gdn_norm=row(p["gdn_norm"][i]),
        ssd_conv=conv_pad(p["ssd_conv"][i]), ssd_conv_b=row(p["ssd_conv_b"][i]),
        ssd_alog=_lane_row(p["ssd_A_log"][i].reshape(-1), SSD_DT_LANE),
        ssd_bias=_lane_row(p["ssd_dt_bias"][i].reshape(-1), SSD_DT_LANE),
        ssd_dskip=row(jnp.repeat(p["ssd_D"][i], SSD_HEAD_DIM)), ssd_norm=row(p["ssd_norm"][i]),
        mla_q_norm=row(p["mla_q_norm"][i]), mla_kv_norm=row(p["mla_kv_norm"][i]),
        mla_wq=wq.astype(BF16), mla_wkv=p["mla_w_ukv"][i].astype(BF16),
        w_branch_a=p["w_branch_a"][i].astype(BF16), w_branch_b=p["w_branch_b"][i].astype(BF16),
        w_branch_c=p["w_branch_c"][i].astype(BF16), w_gate=p["w_gate"][i].astype(BF16),
        b_gate=row(p["b_gate"][i]), w_out=p["w_out"][i].astype(BF16),
        ffn2_norm=row(p["ffn2_norm"][i]), w_ffn2_in=p["w_ffn2_in"][i].astype(BF16),
        w_ffn2_out=p["w_ffn2_out"][i].astype(BF16),
    )


def _rope_table(L):
    inv_freq = jnp.power(ROPE_BASE, -jnp.arange(0, MLA_ROPE, 2, dtype=F32) / MLA_ROPE)
    ang = jnp.arange(L, dtype=F32)[:, None] * inv_freq[None, :]
    ang = jnp.concatenate([ang, ang], axis=-1)
    return jnp.concatenate([jnp.cos(ang), jnp.sin(ang)], axis=-1)


def _rows_to_chunk_rows(cols, B, L):
    n = cols.shape[-1]
    return cols.reshape(B, L // CHUNK, CHUNK, n).transpose(0, 1, 3, 2)


def _layer(x2d, B, L, cs, lp, final_gain, final):
    T = B * L
    h = _ffn(x2d, lp["ffn1_norm"], lp["w_ffn1_in"], lp["w_ffn1_out"], final_gain, False)
    q, k, v, gb, za, zb, xbc_act, dt, ac, qc, kc, vc = _mix(h.reshape(B, L, D_MODEL), cs, lp)

    grow = _rows_to_chunk_rows(gb[..., :GDN_CHAINS], B, L)
    o_f, o_b = _gdn_scan(q, k, v, gb, grow)

    acrow = _rows_to_chunk_rows(ac[..., SSD_DT_LANE:SSD_DT_LANE + 2 * SSD_HEADS], B, L)
    acrow = acrow.reshape(B, L // CHUNK, 2, 1, SSD_HEADS, 1, CHUNK)
    acrow = jnp.broadcast_to(acrow, (B, L // CHUNK, 2, 1, SSD_HEADS, SSD_HEAD_DIM // CHUNK, CHUNK))
    acrow = acrow.reshape(B, L // CHUNK, 2, 1, SSD_INNER)
    y_f, y_b = _ssd_scan(xbc_act, dt, ac, acrow[:, :, 0], acrow[:, :, 1])

    oc = _mla_attn(qc, kc, vc)

    flat = lambda a: a.reshape(T, a.shape[-1])
    h = _merge(h, flat(o_f), flat(o_b), flat(za), flat(y_f), flat(y_b), flat(xbc_act), flat(zb), flat(oc),
               lp["mix_norm"], lp["gdn_norm"], lp["ssd_dskip"], lp["ssd_norm"],
               lp["w_branch_a"], lp["w_branch_b"], lp["w_branch_c"], lp["w_gate"], lp["b_gate"], lp["w_out"])
    return _ffn(h, lp["ffn2_norm"], lp["w_ffn2_in"], lp["w_ffn2_out"], final_gain, final)


def _trunk(x, layers, final_gain):
    B, L, _ = x.shape
    cs = _rope_table(L)
    h = x.reshape(B * L, D_MODEL)
    for i, lp in enumerate(layers):
        h = _layer(h, B, L, cs, lp, final_gain, i == len(layers) - 1)
    return h.reshape(B, L, D_MODEL)


def kernel(x_prompt, x_sample, ffn1_norm, w_ffn1_in, w_ffn1_out, mix_norm, w_in, gdn_conv, gdn_A_log, gdn_dt_bias, gdn_norm, ssd_conv, ssd_conv_b, ssd_A_log, ssd_dt_bias, ssd_D, ssd_norm, mla_q_norm, mla_w_uq, mla_kv_norm, mla_w_ukv, w_branch_a, w_branch_b, w_branch_c, w_gate, b_gate, w_out, ffn2_norm, w_ffn2_in, w_ffn2_out, final_norm):
    p = dict(ffn1_norm=ffn1_norm, w_ffn1_in=w_ffn1_in, w_ffn1_out=w_ffn1_out, mix_norm=mix_norm, w_in=w_in,
             gdn_conv=gdn_conv, gdn_A_log=gdn_A_log, gdn_dt_bias=gdn_dt_bias, gdn_norm=gdn_norm,
             ssd_conv=ssd_conv, ssd_conv_b=ssd_conv_b, ssd_A_log=ssd_A_log, ssd_dt_bias=ssd_dt_bias,
             ssd_D=ssd_D, ssd_norm=ssd_norm, mla_q_norm=mla_q_norm, mla_w_uq=mla_w_uq,
             mla_kv_norm=mla_kv_norm, mla_w_ukv=mla_w_ukv, w_branch_a=w_branch_a, w_branch_b=w_branch_b,
             w_branch_c=w_branch_c, w_gate=w_gate, b_gate=b_gate, w_out=w_out, ffn2_norm=ffn2_norm,
             w_ffn2_in=w_ffn2_in, w_ffn2_out=w_ffn2_out)
    layers = [_prep_layer(p, i) for i in range(DEPTH)]
    final_gain = final_norm.reshape(1, D_MODEL).astype(F32)
    return (_trunk(x_prompt, layers, final_gain), _trunk(x_sample, layers, final_gain))
```

```python
import functools

import jax
import jax.numpy as jnp
from jax import lax
from jax.experimental import pallas as pl
from jax.experimental.pallas import tpu as pltpu

F32 = jnp.float32
BF16 = jnp.bfloat16

D_MODEL = 1024
DEPTH = 2
EPS = 1e-6
CONV_W = 5
CONV_HALF = CONV_W // 2
CHUNK = 64

GDN_HEADS = 4
GDN_DK = 128
GDN_DV = 128
GDN_QK_W = GDN_HEADS * GDN_DK
GDN_V_W = GDN_HEADS * GDN_DV
GDN_QKV_W = 2 * GDN_QK_W + GDN_V_W
GDN_CHAINS = 2 * GDN_HEADS

SSD_HEADS = 8
SSD_HEAD_DIM = 64
SSD_INNER = SSD_HEADS * SSD_HEAD_DIM
SSD_GROUPS = 2
SSD_STATE = 128
SSD_XBC = SSD_INNER + 2 * SSD_GROUPS * SSD_STATE
SSD_GROUP_W = SSD_INNER // SSD_GROUPS
SSD_HEADS_PER_GROUP = SSD_HEADS // SSD_GROUPS
HEAD_SHIFT = 6
assert (1 << HEAD_SHIFT) == SSD_HEAD_DIM == CHUNK

MLA_HEADS = 4
MLA_Q_LORA = 256
MLA_KV_LORA = 256
MLA_NOPE = 128
MLA_ROPE = 64
MLA_V = 128
MLA_QK_PAD = 256
MLA_SCALE = (MLA_NOPE + MLA_ROPE) ** -0.5
MLA_V_PAD = 256
LOG2_E = 1.4426950408889634
ROPE_BASE = 10000.0

FFN_HIDDEN = 2816
N_BRANCH = 3

CONV_GROUP_W = 512
SMALL_W = 256
LANE = 128
SUBLANE = 8
GDN_AB_LANE = 0
SSD_DT_LANE = 16

VMEM_LIMIT = 56 * 1024 * 1024

TM_FFN = 512
TM_MIX = 512
TM_MERGE = 512
TR_SEQ = 512
TQ_ATTN = 512
TK_ATTN = 512
ATTN_KV_VMEM_BUDGET = 32 * 1024 * 1024


def _cparams(*sem):
    return pltpu.CompilerParams(dimension_semantics=sem, vmem_limit_bytes=VMEM_LIMIT)


def _const_spec(shape):
    nd = len(shape)
    return pl.BlockSpec(shape, lambda *_: (0,) * nd, pipeline_mode=pl.Buffered(1))


def _rms(x, gain):
    return x * lax.rsqrt(jnp.mean(x * x, axis=-1, keepdims=True) + EPS) * gain


def _silu(x):
    t = 0.5 * x
    return t + t * jnp.tanh(t)


def _softplus(x):
    return jnp.maximum(x, 0.0) + jnp.log1p(jnp.exp(-jnp.abs(x)))


def _dot(a, b):
    return jnp.dot(a, b, preferred_element_type=F32)


def _dot_nt(a, b):
    return lax.dot_general(a, b, (((1,), (1,)), ((), ())), preferred_element_type=F32)


def _dot_tn(a, b):
    return lax.dot_general(a, b, (((0,), (0,)), ((), ())), preferred_element_type=F32)


def _ffn_kernel(x_ref, g_ref, win_ref, wout_ref, fin_ref, o_ref, *, final):
    x = x_ref[...]
    xn = _rms(x, g_ref[...]).astype(BF16)
    gu = _dot(xn, win_ref[...])
    act = (_silu(gu[:, :FFN_HIDDEN]) * gu[:, FFN_HIDDEN:]).astype(BF16)
    h = x + 0.5 * _dot(act, wout_ref[...])
    if final:
        h = _rms(h, fin_ref[...])
    o_ref[...] = h


def _ffn(x2d, gain, w_in, w_out, fin, final):
    T = x2d.shape[0]
    tm = TM_FFN
    row = pl.BlockSpec((tm, D_MODEL), lambda i: (i, 0))
    return pl.pallas_call(
        functools.partial(_ffn_kernel, final=final),
        grid=(T // tm,),
        in_specs=[row, _const_spec((1, D_MODEL)), _const_spec((D_MODEL, 2 * FFN_HIDDEN)),
                  _const_spec((FFN_HIDDEN, D_MODEL)), _const_spec((1, D_MODEL))],
        out_specs=row,
        out_shape=jax.ShapeDtypeStruct((T, D_MODEL), F32),
        compiler_params=_cparams("parallel"),
        name="ffn",
    )(x2d, gain, w_in, w_out, fin)


def _mix_kernel(cur_ref, prev_ref, next_ref, cs_ref, g_ref, wconv_ref, wrest_ref, wsmall_ref,
                gconv_ref, galog_ref, gbias_ref, sconv_ref, sconvb_ref, salog_ref, sbias_ref,
                qn_ref, kvn_ref, wq_ref, wkv_ref,
                q_ref, k_ref, v_ref, gb_ref, za_ref, zb_ref, xbc_ref, ac_ref,
                xrf_ref, xrb_ref, acxf_ref, acxb_ref, mq_ref, mk_ref, mv_ref):
    i = pl.program_id(1)
    tm = cur_ref.shape[1]
    body = slice(SUBLANE, SUBLANE + tm)
    first = i == 0
    last = i == pl.num_programs(1) - 1
    hx = jnp.concatenate([prev_ref[0], cur_ref[0], next_ref[0]], axis=0)
    u = _rms(hx, g_ref[...]).astype(BF16)

    def conv_cols(c0, w_ref, wc0):
        cols = _dot(u, wconv_ref[:, c0:c0 + CONV_GROUP_W])
        xe = jnp.concatenate([jnp.where(first, 0.0, cols[:SUBLANE]), cols[body],
                              jnp.where(last, 0.0, cols[SUBLANE + tm:])], axis=0)
        return _conv(xe, w_ref, wc0, CONV_GROUP_W)

    for c0, o_ref, scale in ((0, q_ref, GDN_DK ** -0.5), (GDN_QK_W, k_ref, 1.0)):
        a = _silu(conv_cols(c0, gconv_ref, c0))
        for h in range(GDN_HEADS):
            sl = slice(h * GDN_DK, (h + 1) * GDN_DK)
            seg = a[:, sl]
            o_ref[0, :, sl] = (seg * (lax.rsqrt(jnp.sum(seg * seg, axis=-1, keepdims=True) + EPS)
                                      * scale)).astype(BF16)
    v_ref[0] = _silu(conv_cols(2 * GDN_QK_W, gconv_ref, 2 * GDN_QK_W)).astype(BF16)

    small = _dot(u, wsmall_ref[...])[body]

    s = small[:, :LANE]
    lane = lax.broadcasted_iota(jnp.int32, s.shape, 1)
    is_gdn = lane < GDN_CHAINS
    is_ssd = (lane >= SSD_DT_LANE) & (lane < SSD_DT_LANE + 2 * SSD_HEADS)
    sp = _softplus(s + gbias_ref[...] + sbias_ref[...])
    log_decay = jnp.where(is_gdn | is_ssd, -jnp.exp(galog_ref[...] + salog_ref[...]) * sp, 0.0)
    fwd = (lane < GDN_HEADS) | ((lane >= SSD_DT_LANE) & (lane < SSD_DT_LANE + SSD_HEADS))
    csum = _chunk_cumsum(log_decay, fwd)
    gb_ref[0] = jnp.where(is_gdn, csum, jax.nn.sigmoid(s))
    ac_ref[0] = jnp.where(is_ssd, csum, 0.0)

    assert CONV_GROUP_W == SSD_INNER
    for c0 in range(0, SSD_XBC, CONV_GROUP_W):
        a = _silu(conv_cols(GDN_QKV_W + c0, sconv_ref, c0) + sconvb_ref[:, c0:c0 + CONV_GROUP_W])
        xbc_ref[0, :, c0:c0 + CONV_GROUP_W] = a.astype(BF16)
        if c0 == 0:
            for d, (xr_ref, acx_ref) in enumerate(((xrf_ref, acxf_ref), (xrb_ref, acxb_ref))):
                base = SSD_DT_LANE + d * SSD_HEADS
                xr_ref[0] = (a * _expand_heads(sp, base)).astype(BF16)
                acx_ref[0] = _expand_heads(csum, base)

    rest = _dot(u, wrest_ref[...])[body]
    za_ref[0] = rest[:, :GDN_V_W].astype(BF16)
    zb_ref[0] = rest[:, GDN_V_W:GDN_V_W + SSD_INNER].astype(BF16)

    c = rest[:, GDN_V_W + SSD_INNER:]
    cs = cs_ref[...]
    cq = _rms(c[:, :MLA_Q_LORA], qn_ref[...]).astype(BF16)
    ckv = _rms(c[:, MLA_Q_LORA:], kvn_ref[...]).astype(BF16)
    qa = _dot(cq, wq_ref[...]) * (MLA_SCALE * LOG2_E)
    kva = _dot(ckv, wkv_ref[...])
    k_rope = _rope_pair(small[:, LANE:], cs).astype(BF16)
    for h in range(MLA_HEADS):
        base = h * MLA_QK_PAD
        mq_ref[0, h, :, :MLA_NOPE] = qa[:, base:base + MLA_NOPE].astype(BF16)
        mq_ref[0, h, :, MLA_NOPE:] = _rope_pair(qa[:, base + MLA_NOPE:base + MLA_QK_PAD], cs).astype(BF16)
        mk_ref[0, h, :, :MLA_NOPE] = kva[:, base:base + MLA_NOPE].astype(BF16)
        mk_ref[0, h, :, MLA_NOPE:] = k_rope
        mv_ref[0, h, :, :MLA_V] = kva[:, base + MLA_NOPE:base + MLA_NOPE + MLA_V].astype(BF16)
        mv_ref[0, h, :, MLA_V:] = jnp.ones((tm, MLA_V_PAD - MLA_V), BF16)


def _mix(h3, cs, lp):
    B, L, _ = h3.shape
    tm = TM_MIX
    cur, prev, nxt = _seq_specs(tm, D_MODEL, L)
    out = lambda w: pl.BlockSpec((1, tm, w), lambda b, i: (b, i, 0))
    head_out = lambda w: pl.BlockSpec((1, MLA_HEADS, tm, w), lambda b, i: (b, 0, i, 0))
    seq_shape = lambda w, dt: jax.ShapeDtypeStruct((B, L, w), dt)
    head_shape = lambda w: jax.ShapeDtypeStruct((B, MLA_HEADS, L, w), BF16)
    consts = (lp["mix_norm"], lp["w_conv"], lp["w_rest"], lp["w_small"],
              lp["gdn_conv"], lp["gdn_alog"], lp["gdn_bias"],
              lp["ssd_conv"], lp["ssd_conv_b"], lp["ssd_alog"], lp["ssd_bias"],
              lp["mla_q_norm"], lp["mla_kv_norm"], lp["mla_wq"], lp["mla_wkv"])
    return pl.pallas_call(
        _mix_kernel,
        grid=(B, L // tm),
        in_specs=[cur, prev, nxt, pl.BlockSpec((tm, LANE), lambda b, i: (i, 0))]
        + [_const_spec(a.shape) for a in consts],
        out_specs=[out(GDN_QK_W), out(GDN_QK_W), out(GDN_V_W), out(LANE), out(GDN_V_W), out(SSD_INNER),
                   out(SSD_XBC), out(LANE),
                   out(SSD_INNER), out(SSD_INNER), out(SSD_INNER), out(SSD_INNER),
                   head_out(MLA_QK_PAD), head_out(MLA_QK_PAD), head_out(MLA_V_PAD)],
        out_shape=[seq_shape(GDN_QK_W, BF16), seq_shape(GDN_QK_W, BF16), seq_shape(GDN_V_W, BF16),
                   seq_shape(LANE, F32), seq_shape(GDN_V_W, BF16), seq_shape(SSD_INNER, BF16),
                   seq_shape(SSD_XBC, BF16), seq_shape(LANE, F32),
                   seq_shape(SSD_INNER, BF16), seq_shape(SSD_INNER, BF16),
                   seq_shape(SSD_INNER, F32), seq_shape(SSD_INNER, F32),
                   head_shape(MLA_QK_PAD), head_shape(MLA_QK_PAD), head_shape(MLA_V_PAD)],
        compiler_params=_cparams("parallel", "parallel"),
        name="mix",
    )(h3, h3, h3, cs, *consts)


def _seq_specs(tr, width, L):
    per = tr // SUBLANE
    last = L // SUBLANE - 1
    cur = pl.BlockSpec((1, tr, width), lambda b, i: (b, i, 0))
    prev = pl.BlockSpec((1, SUBLANE, width), lambda b, i: (b, jnp.maximum(i * per - 1, 0), 0))
    nxt = pl.BlockSpec((1, SUBLANE, width), lambda b, i: (b, jnp.minimum((i + 1) * per, last), 0))
    return cur, prev, nxt


def _conv(xe, w_ref, c0, width):
    n = xe.shape[0]
    groups = (n - 2 * SUBLANE) // SUBLANE
    x3 = xe.reshape(n // SUBLANE, SUBLANE, width)
    sub = lax.broadcasted_iota(jnp.int32, (groups, SUBLANE, width), 1)
    acc = None
    for k in range(CONV_W):
        s = CONV_HALF - k
        if s == 0:
            shifted = x3[1:1 + groups]
        else:
            r = pltpu.roll(x3, s % SUBLANE, 1)
            if s > 0:
                shifted = jnp.where(sub < s, r[0:groups], r[1:1 + groups])
            else:
                shifted = jnp.where(sub >= SUBLANE + s, r[2:2 + groups], r[1:1 + groups])
        term = shifted * w_ref[k:k + 1, c0:c0 + width]
        acc = term if acc is None else acc + term
    return acc.reshape(groups * SUBLANE, width)


def _chunk_cumsum(g, lane_is_fwd):
    tr = g.shape[0]
    pos = lax.broadcasted_iota(jnp.int32, g.shape, 0) & (CHUNK - 1)
    pre = g
    suf = g
    s = 1
    while s < CHUNK:
        pre = pre + jnp.where(pos >= s, pltpu.roll(pre, s, 0), 0.0)
        suf = suf + jnp.where(pos < CHUNK - s, pltpu.roll(suf, tr - s, 0), 0.0)
        s *= 2
    return jnp.where(lane_is_fwd, pre, suf)


NEUMANN_STEPS = 6
assert 2 ** NEUMANN_STEPS == CHUNK


def _gdn_scan_kernel(qf_ref, kf_ref, vf_ref, gbf_ref, grf_ref, qb_ref, kb_ref, vb_ref, gbb_ref, grb_ref,
                     of_ref, ob_ref, s_ref, *, nc):
    @pl.when(pl.program_id(1) == 0)
    def _():
        s_ref[...] = jnp.zeros_like(s_ref)

    dirs = ((qf_ref, kf_ref, vf_ref, gbf_ref, grf_ref, of_ref),
            (qb_ref, kb_ref, vb_ref, gbb_ref, grb_ref, ob_ref))
    ii = lax.broadcasted_iota(jnp.int32, (CHUNK, CHUNK), 0)
    jj = lax.broadcasted_iota(jnp.int32, (CHUNK, CHUNK), 1)
    eye = jnp.where(ii == jj, 1.0, 0.0)
    t_half = lax.broadcasted_iota(jnp.int32, (CHUNK, 2 * CHUNK), 1) >= CHUNK

    units = []
    for step_idx in range(nc):
        for d, (q_ref, k_ref, v_ref, gb_ref, gr_ref, _) in enumerate(dirs):
            cc = step_idx if d == 0 else nc - 1 - step_idx
            rows = slice(cc * CHUNK, (cc + 1) * CHUNK)
            gb = gb_ref[0, rows, :]
            grow_all = gr_ref[0, cc]
            for h in range(GDN_HEADS):
                j = d * GDN_HEADS + h
                sl = slice(h * GDN_DK, (h + 1) * GDN_DK)
                units.append(dict(d=d, j=j, rows=rows, sl=sl, q=q_ref[0, rows, sl], k=k_ref[0, rows, sl],
                                  v=v_ref[0, rows, sl], gcol=gb[:, j:j + 1], grow=grow_all[j:j + 1, :],
                                  bcol=gb[:, GDN_CHAINS + j:GDN_CHAINS + j + 1]))
    for t in units:
        r = _dot_nt(jnp.concatenate([t["k"], t["q"]], axis=0), t["k"])
        t["kk"] = r[:CHUNK]
        t["qk"] = r[CHUNK:]
    for t in units:
        gcol, bcol = t["gcol"], t["bcol"]
        incl = (ii >= jj) if t["d"] == 0 else (ii <= jj)
        dec = jnp.where(incl, jnp.exp(jnp.minimum(gcol - t["grow"], 0.0)), 0.0)
        nmat = -(bcol * t.pop("kk")) * jnp.where(ii == jj, 0.0, dec)
        t["nt"] = jnp.concatenate([nmat, eye], axis=1)
        t["attn"] = (t.pop("qk") * dec).astype(BF16)
        kf = t["k"].astype(F32)
        eg = jnp.exp(gcol)
        glast = gcol[CHUNK - 1:CHUNK] if t["d"] == 0 else gcol[0:1]
        t["rhs"] = (bcol * jnp.concatenate([t["v"].astype(F32), kf * eg], axis=1)).astype(BF16)
        t["qd"] = t["q"].astype(F32) * eg
        t["kd"] = (kf * jnp.exp(glast - gcol)).astype(BF16)
        t["eglast"] = jnp.exp(glast)
    for step in range(NEUMANN_STEPS):
        for t in units:
            ntb = t["nt"].astype(BF16)
            t["y"] = _dot(ntb[:, :CHUNK], ntb)
        for t in units:
            t["nt"] = t.pop("y") + jnp.where(t_half, t["nt"], 0.0)
    for t in units:
        x = _dot(t.pop("nt")[:, CHUNK:].astype(BF16), t.pop("rhs"))
        t["u"] = x[:, :GDN_DV]
        t["wq"] = jnp.concatenate([x[:, GDN_DV:], t.pop("qd")], axis=0).astype(BF16)

    state = [s_ref[j] for j in range(GDN_CHAINS)]
    for step_idx in range(nc):
        group = units[step_idx * GDN_CHAINS:(step_idx + 1) * GDN_CHAINS]
        for t in group:
            t["r"] = _dot(t["wq"], state[t["j"]].astype(BF16))
        for t in group:
            t["vnew"] = (t["u"] - t["r"][:CHUNK]).astype(BF16)
        for t in group:
            t["o"] = t["r"][CHUNK:] + _dot(t["attn"], t["vnew"])
            state[t["j"]] = state[t["j"]] * t["eglast"] + _dot_tn(t["kd"], t["vnew"])
        for t in group:
            dirs[t["d"]][5][0, t["rows"], t["sl"]] = t["o"]
    for j in range(GDN_CHAINS):
        s_ref[j] = state[j]


def _gdn_scan(q, k, v, gb, grow):
    B, L, _ = q.shape
    tr = TR_SEQ
    nc = tr // CHUNK
    nt = L // tr
    fwd = lambda w: pl.BlockSpec((1, tr, w), lambda b, i: (b, i, 0))
    bwd = lambda w: pl.BlockSpec((1, tr, w), lambda b, i: (b, nt - 1 - i, 0))
    grf = pl.BlockSpec((1, nc, GDN_CHAINS, CHUNK), lambda b, i: (b, i, 0, 0))
    grb = pl.BlockSpec((1, nc, GDN_CHAINS, CHUNK), lambda b, i: (b, nt - 1 - i, 0, 0))
    return pl.pallas_call(
        functools.partial(_gdn_scan_kernel, nc=nc),
        grid=(B, nt),
        in_specs=[fwd(GDN_QK_W), fwd(GDN_QK_W), fwd(GDN_V_W), fwd(LANE), grf,
                  bwd(GDN_QK_W), bwd(GDN_QK_W), bwd(GDN_V_W), bwd(LANE), grb],
        out_specs=[fwd(GDN_V_W), bwd(GDN_V_W)],
        out_shape=[jax.ShapeDtypeStruct((B, L, GDN_V_W), F32)] * 2,
        scratch_shapes=[pltpu.VMEM((GDN_CHAINS, GDN_DK, GDN_DV), F32)],
        compiler_params=_cparams("arbitrary", "arbitrary"),
        name="gdn_scan",
    )(q, k, v, gb, grow, q, k, v, gb, grow)


def _expand_heads(cols, base):
    rows = cols.shape[0]
    low_half = lax.broadcasted_iota(jnp.int32, (rows, LANE), 1) < SSD_HEAD_DIM
    tiles = []
    for h in range(0, SSD_HEADS, LANE // SSD_HEAD_DIM):
        tiles.append(jnp.where(low_half, cols[:, base + h:base + h + 1], cols[:, base + h + 1:base + h + 2]))
    return jnp.concatenate(tiles, axis=1)


def _ssd_scan_kernel(xf_ref, xrf_ref, acf_ref, arf_ref, xb_ref, xrb_ref, acb_ref, arb_ref,
                     yf_ref, yb_ref, h_ref, *, nc):
    @pl.when(pl.program_id(1) == 0)
    def _():
        h_ref[...] = jnp.zeros_like(h_ref)

    dirs = ((xf_ref, xrf_ref, acf_ref, arf_ref, yf_ref), (xb_ref, xrb_ref, acb_ref, arb_ref, yb_ref))
    ii = lax.broadcasted_iota(jnp.int32, (CHUNK, SSD_GROUP_W), 0)
    jj = lax.broadcasted_iota(jnp.int32, (CHUNK, SSD_GROUP_W), 1) & (CHUNK - 1)
    rb = lax.broadcasted_iota(jnp.int32, (SSD_GROUP_W, SSD_GROUP_W), 0) >> HEAD_SHIFT
    cb_ = lax.broadcasted_iota(jnp.int32, (SSD_GROUP_W, SSD_GROUP_W), 1) >> HEAD_SHIFT
    block_diag = rb == cb_

    units = []
    for step_idx in range(nc):
        for d, (x_ref, xr_ref, ac_ref, ar_ref, _) in enumerate(dirs):
            cc = step_idx if d == 0 else nc - 1 - step_idx
            rows = slice(cc * CHUNK, (cc + 1) * CHUNK)
            acx = ac_ref[0, rows, :]
            xr = xr_ref[0, rows, :].astype(F32)
            aclast = acx[CHUNK - 1:CHUNK] if d == 0 else acx[0:1]
            acrow = ar_ref[0, cc]
            for g in range(SSD_GROUPS):
                sl = slice(g * SSD_GROUP_W, (g + 1) * SSD_GROUP_W)
                b0 = SSD_INNER + g * SSD_STATE
                c0 = SSD_INNER + (SSD_GROUPS + g) * SSD_STATE
                units.append(dict(d=d, g=g, rows=rows, sl=sl, bg=x_ref[0, rows, b0:b0 + SSD_STATE],
                                  cg=x_ref[0, rows, c0:c0 + SSD_STATE], acx=acx[:, sl], acrow=acrow[:, sl],
                                  aclast=aclast[:, sl], xr=xr[:, sl]))
    for t in units:
        t["cbt"] = _dot_nt(t["cg"], jnp.concatenate([t["bg"]] * SSD_HEADS_PER_GROUP, axis=0))
    for t in units:
        incl = (ii >= jj) if t["d"] == 0 else (ii <= jj)
        seg = jnp.where(incl, jnp.exp(jnp.minimum(t["acx"] - t["acrow"], 0.0)), 0.0)
        m = (t.pop("cbt") * seg).astype(BF16)
        xr = t.pop("xr")
        xr_bd = jnp.where(block_diag, jnp.concatenate([xr] * SSD_HEADS_PER_GROUP, axis=0), 0.0).astype(BF16)
        t["ydiag"] = _dot(m, xr_bd)
        t["st"] = _dot_tn(t["bg"], (xr * jnp.exp(t["aclast"] - t["acx"])).astype(BF16))

    state = {(d, g): h_ref[d, g] for d in range(2) for g in range(SSD_GROUPS)}
    per_step = 2 * SSD_GROUPS
    for step_idx in range(nc):
        for t in units[step_idx * per_step:(step_idx + 1) * per_step]:
            key = (t["d"], t["g"])
            y = t["ydiag"] + _dot(t["cg"], state[key].astype(BF16)) * jnp.exp(t["acx"])
            dirs[t["d"]][4][0, t["rows"], t["sl"]] = y
            state[key] = state[key] * jnp.exp(t["aclast"]) + t["st"]
    for (d, g), hval in state.items():
        h_ref[d, g] = hval


def _ssd_scan(xbc, xr_f, xr_b, acx_f, acx_b, acrow_f, acrow_b):
    B, L, _ = xbc.shape
    tr = TR_SEQ
    nc = tr // CHUNK
    nt = L // tr
    fwd = lambda w: pl.BlockSpec((1, tr, w), lambda b, i: (b, i, 0))
    bwd = lambda w: pl.BlockSpec((1, tr, w), lambda b, i: (b, nt - 1 - i, 0))
    arf = pl.BlockSpec((1, nc, 1, SSD_INNER), lambda b, i: (b, i, 0, 0))
    arb = pl.BlockSpec((1, nc, 1, SSD_INNER), lambda b, i: (b, nt - 1 - i, 0, 0))
    return pl.pallas_call(
        functools.partial(_ssd_scan_kernel, nc=nc),
        grid=(B, nt),
        in_specs=[fwd(SSD_XBC), fwd(SSD_INNER), fwd(SSD_INNER), arf,
                  bwd(SSD_XBC), bwd(SSD_INNER), bwd(SSD_INNER), arb],
        out_specs=[fwd(SSD_INNER), bwd(SSD_INNER)],
        out_shape=[jax.ShapeDtypeStruct((B, L, SSD_INNER), F32)] * 2,
        scratch_shapes=[pltpu.VMEM((2, SSD_GROUPS, SSD_STATE, SSD_GROUP_W), F32)],
        compiler_params=_cparams("arbitrary", "arbitrary"),
        name="ssd_scan",
    )(xbc, xr_f, acx_f, acrow_f, xbc, xr_b, acx_b, acrow_b)


def _rope_pair(pair, cs):
    prod = pair * cs
    lane = lax.broadcasted_iota(jnp.int32, prod.shape, 1)
    return jnp.where(lane < MLA_ROPE, prod + pltpu.roll(prod, MLA_ROPE, 1), 0.0)


def _mla_attn_kernel(q_ref, k_ref, v_ref, o_ref):
    hs, tq = q_ref.shape[1], q_ref.shape[2]
    nk = k_ref.shape[2] // TK_ATTN
    steps = [(h, j) for h in range(hs) for j in range(nk)]

    def scores(h, j):
        return _dot_nt(q_ref[0, h], k_ref[0, h, j * TK_ATTN:(j + 1) * TK_ATTN, :])

    s_next = scores(*steps[0])
    for idx, (h, j) in enumerate(steps):
        s = s_next
        if idx + 1 < len(steps):
            s_next = scores(*steps[idx + 1])
        if j == 0:
            m = jnp.full((tq, 1), -jnp.inf, F32)
            acc = jnp.zeros((tq, MLA_V_PAD), F32)
        m_new = jnp.maximum(m, jnp.max(s, axis=-1, keepdims=True))
        p = jnp.exp2(s - m_new).astype(BF16)
        acc = jnp.exp2(m - m_new) * acc + _dot(p, v_ref[0, h, j * TK_ATTN:(j + 1) * TK_ATTN, :])
        m = m_new
        if j == nk - 1:
            o_ref[0, :, h * MLA_V:(h + 1) * MLA_V] = (acc[:, :MLA_V] / acc[:, MLA_V:]).astype(BF16)


def _mla_attn(q, k, v):
    B, H, L, _ = q.shape
    tq = TQ_ATTN
    kv_bytes_per_head = 2 * L * (MLA_QK_PAD + MLA_V_PAD) * 2
    hs = max(1, min(H, ATTN_KV_VMEM_BUDGET // kv_bytes_per_head))
    while H % hs:
        hs -= 1
    return pl.pallas_call(
        _mla_attn_kernel,
        grid=(B, H // hs, L // tq),
        in_specs=[pl.BlockSpec((1, hs, tq, MLA_QK_PAD), lambda b, h, i: (b, h, i, 0)),
                  pl.BlockSpec((1, hs, L, MLA_QK_PAD), lambda b, h, i: (b, h, 0, 0)),
                  pl.BlockSpec((1, hs, L, MLA_V_PAD), lambda b, h, i: (b, h, 0, 0))],
        out_specs=pl.BlockSpec((1, tq, hs * MLA_V), lambda b, h, i: (b, i, h)),
        out_shape=jax.ShapeDtypeStruct((B, L, H * MLA_V), BF16),
        compiler_params=_cparams("parallel", "parallel", "parallel"),
        name="mla_attn",
    )(q, k, v)


def _merge_kernel(h_ref, of_ref, ob_ref, za_ref, yf_ref, yb_ref, xbc_ref, zb_ref, oc_ref,
                  mixg_ref, gng_ref, dskip_ref, sng_ref, wa_ref, wb_ref, wc_ref, wg_ref, bg_ref, wo_ref,
                  out_ref):
    h = h_ref[...]
    u = _rms(h, mixg_ref[...]).astype(BF16)
    gates = jax.nn.sigmoid(_dot(u, wg_ref[...]) + bg_ref[...])

    o = of_ref[...] + ob_ref[...]
    za = za_ref[...].astype(F32)
    parts = []
    for hd in range(GDN_HEADS):
        sl = slice(hd * GDN_DV, (hd + 1) * GDN_DV)
        parts.append(_rms(o[:, sl], gng_ref[...]) * _silu(za[:, sl]))
    ya = _dot(jnp.concatenate(parts, axis=1).astype(BF16), wa_ref[...])

    y = yf_ref[...] + yb_ref[...] + xbc_ref[...].astype(F32) * dskip_ref[...]
    y = _rms(y * _silu(zb_ref[...].astype(F32)), sng_ref[...])
    yb = _dot(y.astype(BF16), wb_ref[...])

    yc = _dot(oc_ref[...], wc_ref[...])

    merged = (gates[:, :D_MODEL] * ya + gates[:, D_MODEL:2 * D_MODEL] * yb
              + gates[:, 2 * D_MODEL:] * yc)
    out_ref[...] = h + _dot(merged.astype(BF16), wo_ref[...])


def _merge(h2d, o_f, o_b, za, y_f, y_b, xbc_act, zb, oc, mix_gain, gdn_gain, dskip, ssd_gain,
           wa, wb, wc, wg, bg, wo):
    T = h2d.shape[0]
    tm = TM_MERGE
    row = lambda w: pl.BlockSpec((tm, w), lambda i: (i, 0))
    return pl.pallas_call(
        _merge_kernel,
        grid=(T // tm,),
        in_specs=[row(D_MODEL), row(GDN_V_W), row(GDN_V_W), row(GDN_V_W), row(SSD_INNER), row(SSD_INNER),
                  row(SSD_INNER), row(SSD_INNER), row(MLA_HEADS * MLA_V),
                  _const_spec((1, D_MODEL)), _const_spec((1, GDN_DV)), _const_spec((1, SSD_INNER)),
                  _const_spec((1, SSD_INNER)), _const_spec((GDN_V_W, D_MODEL)),
                  _const_spec((SSD_INNER, D_MODEL)), _const_spec((MLA_HEADS * MLA_V, D_MODEL)),
                  _const_spec((D_MODEL, N_BRANCH * D_MODEL)), _const_spec((1, N_BRANCH * D_MODEL)),
                  _const_spec((D_MODEL, D_MODEL))],
        out_specs=row(D_MODEL),
        out_shape=jax.ShapeDtypeStruct((T, D_MODEL), F32),
        compiler_params=_cparams("parallel"),
        name="merge",
    )(h2d, o_f, o_b, za, y_f, y_b, xbc_act, zb, oc, mix_gain, gdn_gain, dskip, ssd_gain,
      wa, wb, wc, wg, bg, wo)


def _rotate_half_cols(w):
    w1, w2 = jnp.split(w, 2, axis=-1)
    return jnp.concatenate([-w2, w1], axis=-1)


def _lane_row(values, base):
    row = jnp.zeros((1, LANE), F32)
    return row.at[0, base:base + values.shape[0]].set(values.astype(F32))


def _prep_layer(p, i):
    w_in = p["w_in"][i]
    c = 0
    seg = {}
    for name, width in (("qkv", GDN_QKV_W), ("za", GDN_V_W), ("ab", 4 * GDN_HEADS), ("zb", SSD_INNER),
                        ("xbc", SSD_XBC), ("dt", 2 * SSD_HEADS), ("cq", MLA_Q_LORA), ("ckv", MLA_KV_LORA),
                        ("kr", MLA_ROPE)):
        seg[name] = w_in[:, c:c + width]
        c += width
    w_conv = jnp.concatenate([seg["qkv"], seg["xbc"]], axis=1)
    w_rest = jnp.concatenate([seg["za"], seg["zb"], seg["cq"], seg["ckv"]], axis=1)
    pad = jnp.zeros((D_MODEL, LANE - 4 * GDN_HEADS - 2 * SSD_HEADS), F32)
    w_small = jnp.concatenate([seg["ab"], seg["dt"], pad, seg["kr"], _rotate_half_cols(seg["kr"])], axis=1)

    wq = p["mla_w_uq"][i].reshape(MLA_Q_LORA, MLA_HEADS, MLA_NOPE + MLA_ROPE)
    wq_rope = wq[..., MLA_NOPE:]
    wq = jnp.concatenate([wq, _rotate_half_cols(wq_rope)], axis=-1).reshape(MLA_Q_LORA, MLA_HEADS * MLA_QK_PAD)

    row = lambda v: v.reshape(1, -1).astype(F32)
    conv_pad = lambda w: jnp.concatenate([w, jnp.zeros((SUBLANE - CONV_W, w.shape[1]), F32)], axis=0)
    return dict(
        ffn1_norm=row(p["ffn1_norm"][i]), w_ffn1_in=p["w_ffn1_in"][i].astype(BF16),
        w_ffn1_out=p["w_ffn1_out"][i].astype(BF16),
        mix_norm=row(p["mix_norm"][i]), w_conv=w_conv.astype(BF16), w_rest=w_rest.astype(BF16),
        w_small=w_small.astype(BF16),
        gdn_conv=conv_pad(p["gdn_conv"][i]),
        gdn_alog=_lane_row(p["gdn_A_log"][i].reshape(-1), GDN_AB_LANE),
        gdn_bias=_lane_row(p["gdn_dt_bias"][i].reshape(-1), GDN_AB_LANE),
        gdn_norm=row(p["gdn_norm"][i]),
        ssd_conv=conv_pad(p["ssd_conv"][i]), ssd_conv_b=row(p["ssd_conv_b"][i]),
        ssd_alog=_lane_row(p["ssd_A_log"][i].reshape(-1), SSD_DT_LANE),
        ssd_bias=_lane_row(p["ssd_dt_bias"][i].reshape(-1), SSD_DT_LANE),
        ssd_dskip=row(jnp.repeat(p["ssd_D"][i], SSD_HEAD_DIM)), ssd_norm=row(p["ssd_norm"][i]),
        mla_q_norm=row(p["mla_q_norm"][i]), mla_kv_norm=row(p["mla_kv_norm"][i]),
        mla_wq=wq.astype(BF16), mla_wkv=p["mla_w_ukv"][i].astype(BF16),
        w_branch_a=p["w_branch_a"][i].astype(BF16), w_branch_b=p["w_branch_b"][i].astype(BF16),
        w_branch_c=p["w_branch_c"][i].astype(BF16), w_gate=p["w_gate"][i].astype(BF16),
        b_gate=row(p["b_gate"][i]), w_out=p["w_out"][i].astype(BF16),
        ffn2_norm=row(p["ffn2_norm"][i]), w_ffn2_in=p["w_ffn2_in"][i].astype(BF16),
        w_ffn2_out=p["w_ffn2_out"][i].astype(BF16),
    )


def _rope_table(L):
    inv_freq = jnp.power(ROPE_BASE, -jnp.arange(0, MLA_ROPE, 2, dtype=F32) / MLA_ROPE)
    ang = jnp.arange(L, dtype=F32)[:, None] * inv_freq[None, :]
    ang = jnp.concatenate([ang, ang], axis=-1)
    return jnp.concatenate([jnp.cos(ang), jnp.sin(ang)], axis=-1)


def _rows_to_chunk_rows(cols, B, L):
    n = cols.shape[-1]
    return cols.reshape(B, L // CHUNK, CHUNK, n).transpose(0, 1, 3, 2)


def _layer(x2d, B, L, cs, lp, final_gain, final):
    T = B * L
    h = _ffn(x2d, lp["ffn1_norm"], lp["w_ffn1_in"], lp["w_ffn1_out"], final_gain, False)
    (q, k, v, gb, za, zb, xbc_act, ac, xr_f, xr_b, acx_f, acx_b,
     qc, kc, vc) = _mix(h.reshape(B, L, D_MODEL), cs, lp)

    grow = _rows_to_chunk_rows(gb[..., :GDN_CHAINS], B, L)
    o_f, o_b = _gdn_scan(q, k, v, gb, grow)

    acrow = _rows_to_chunk_rows(ac[..., SSD_DT_LANE:SSD_DT_LANE + 2 * SSD_HEADS], B, L)
    acrow = acrow.reshape(B, L // CHUNK, 2, 1, SSD_HEADS, 1, CHUNK)
    acrow = jnp.broadcast_to(acrow, (B, L // CHUNK, 2, 1, SSD_HEADS, SSD_HEAD_DIM // CHUNK, CHUNK))
    acrow = acrow.reshape(B, L // CHUNK, 2, 1, SSD_INNER)
    y_f, y_b = _ssd_scan(xbc_act, xr_f, xr_b, acx_f, acx_b, acrow[:, :, 0], acrow[:, :, 1])

    oc = _mla_attn(qc, kc, vc)

    flat = lambda a: a.reshape(T, a.shape[-1])
    h = _merge(h, flat(o_f), flat(o_b), flat(za), flat(y_f), flat(y_b), flat(xbc_act), flat(zb), flat(oc),
               lp["mix_norm"], lp["gdn_norm"], lp["ssd_dskip"], lp["ssd_norm"],
               lp["w_branch_a"], lp["w_branch_b"], lp["w_branch_c"], lp["w_gate"], lp["b_gate"], lp["w_out"])
    return _ffn(h, lp["ffn2_norm"], lp["w_ffn2_in"], lp["w_ffn2_out"], final_gain, final)


def _trunk(x, layers, final_gain):
    B, L, _ = x.shape
    cs = _rope_table(L)
    h = x.reshape(B * L, D_MODEL)
    for i, lp in enumerate(layers):
        h = _layer(h, B, L, cs, lp, final_gain, i == len(layers) - 1)
    return h.reshape(B, L, D_MODEL)


def kernel(x_prompt, x_sample, ffn1_norm, w_ffn1_in, w_ffn1_out, mix_norm, w_in, gdn_conv, gdn_A_log, gdn_dt_bias, gdn_norm, ssd_conv, ssd_conv_b, ssd_A_log, ssd_dt_bias, ssd_D, ssd_norm, mla_q_norm, mla_w_uq, mla_kv_norm, mla_w_ukv, w_branch_a, w_branch_b, w_branch_c, w_gate, b_gate, w_out, ffn2_norm, w_ffn2_in, w_ffn2_out, final_norm):
    p = dict(ffn1_norm=ffn1_norm, w_ffn1_in=w_ffn1_in, w_ffn1_out=w_ffn1_out, mix_norm=mix_norm, w_in=w_in,
             gdn_conv=gdn_conv, gdn_A_log=gdn_A_log, gdn_dt_bias=gdn_dt_bias, gdn_norm=gdn_norm,
             ssd_conv=ssd_conv, ssd_conv_b=ssd_conv_b, ssd_A_log=ssd_A_log, ssd_dt_bias=ssd_dt_bias,
             ssd_D=ssd_D, ssd_norm=ssd_norm, mla_q_norm=mla_q_norm, mla_w_uq=mla_w_uq,
             mla_kv_norm=mla_kv_norm, mla_w_ukv=mla_w_ukv, w_branch_a=w_branch_a, w_branch_b=w_branch_b,
             w_branch_c=w_branch_c, w_gate=w_gate, b_gate=b_gate, w_out=w_out, ffn2_norm=ffn2_norm,
             w_ffn2_in=w_ffn2_in, w_ffn2_out=w_ffn2_out)
    layers = [_prep_layer(p, i) for i in range(DEPTH)]
    final_gain = final_norm.reshape(1, D_MODEL).astype(F32)
    return (_trunk(x_prompt, layers, final_gain), _trunk(x_sample, layers, final_gain))
```

```python
import functools

import jax
import jax.numpy as jnp
from jax import lax
from jax.experimental import pallas as pl
from jax.experimental.pallas import tpu as pltpu

F32 = jnp.float32
BF16 = jnp.bfloat16

D_MODEL = 1024
DEPTH = 2
EPS = 1e-6
CONV_W = 5
CONV_HALF = CONV_W // 2
CHUNK = 64

GDN_HEADS = 4
GDN_DK = 128
GDN_DV = 128
GDN_QK_W = GDN_HEADS * GDN_DK
GDN_V_W = GDN_HEADS * GDN_DV
GDN_QKV_W = 2 * GDN_QK_W + GDN_V_W
GDN_CHAINS = 2 * GDN_HEADS

SSD_HEADS = 8
SSD_HEAD_DIM = 64
SSD_INNER = SSD_HEADS * SSD_HEAD_DIM
SSD_GROUPS = 2
SSD_STATE = 128
SSD_XBC = SSD_INNER + 2 * SSD_GROUPS * SSD_STATE
SSD_GROUP_W = SSD_INNER // SSD_GROUPS
SSD_HEADS_PER_GROUP = SSD_HEADS // SSD_GROUPS
HEAD_SHIFT = 6
assert (1 << HEAD_SHIFT) == SSD_HEAD_DIM == CHUNK

MLA_HEADS = 4
MLA_Q_LORA = 256
MLA_KV_LORA = 256
MLA_NOPE = 128
MLA_ROPE = 64
MLA_V = 128
MLA_QK_PAD = 256
MLA_SCALE = (MLA_NOPE + MLA_ROPE) ** -0.5
MLA_V_PAD = 256
LOG2_E = 1.4426950408889634
ROPE_BASE = 10000.0

FFN_HIDDEN = 2816
N_BRANCH = 3

CONV_GROUP_W = 512
LANE = 128
SUBLANE = 8
GDN_AB_LANE = 0
SSD_DT_LANE = 16

VMEM_LIMIT = 56 * 1024 * 1024

TM_FFN = 512
TM_MIX = 512
TM_MERGE = 512
TR_SEQ = 512
TQ_ATTN = 1024
TK_ATTN = 512
ATTN_KV_VMEM_BUDGET = 32 * 1024 * 1024


def _cparams(*sem):
    return pltpu.CompilerParams(dimension_semantics=sem, vmem_limit_bytes=VMEM_LIMIT)


def _const_spec(shape):
    nd = len(shape)
    return pl.BlockSpec(shape, lambda *_: (0,) * nd, pipeline_mode=pl.Buffered(1))


def _rms(x, gain):
    return x * lax.rsqrt(jnp.mean(x * x, axis=-1, keepdims=True) + EPS) * gain


def _silu(x):
    t = 0.5 * x
    return t + t * jnp.tanh(t)


def _softplus(x):
    return jnp.maximum(x, 0.0) + jnp.log1p(jnp.exp(-jnp.abs(x)))


def _dot(a, b):
    return jnp.dot(a, b, preferred_element_type=F32)


def _dot_nt(a, b):
    return lax.dot_general(a, b, (((1,), (1,)), ((), ())), preferred_element_type=F32)


def _dot_tn(a, b):
    return lax.dot_general(a, b, (((0,), (0,)), ((), ())), preferred_element_type=F32)


def _ffn_kernel(x_ref, g_ref, win_ref, wout_ref, fin_ref, o_ref, *, final):
    x = x_ref[...]
    xn = _rms(x, g_ref[...]).astype(BF16)
    gu = _dot(xn, win_ref[...])
    act = (_silu(gu[:, :FFN_HIDDEN]) * gu[:, FFN_HIDDEN:]).astype(BF16)
    h = x + 0.5 * _dot(act, wout_ref[...])
    if final:
        h = _rms(h, fin_ref[...])
    o_ref[...] = h


def _ffn(x2d, gain, w_in, w_out, fin, final):
    T = x2d.shape[0]
    tm = TM_FFN
    row = pl.BlockSpec((tm, D_MODEL), lambda i: (i, 0))
    return pl.pallas_call(
        functools.partial(_ffn_kernel, final=final),
        grid=(T // tm,),
        in_specs=[row, _const_spec((1, D_MODEL)), _const_spec((D_MODEL, 2 * FFN_HIDDEN)),
                  _const_spec((FFN_HIDDEN, D_MODEL)), _const_spec((1, D_MODEL))],
        out_specs=row,
        out_shape=jax.ShapeDtypeStruct((T, D_MODEL), F32),
        compiler_params=_cparams("parallel"),
        name="ffn",
    )(x2d, gain, w_in, w_out, fin)


def _mix_kernel(cur_ref, prev_ref, next_ref, cs_ref, g_ref, wconv_ref, wrest_ref, wsmall_ref,
                gconv_ref, galog_ref, gbias_ref, sconv_ref, sconvb_ref, salog_ref, sbias_ref,
                qn_ref, kvn_ref, wq_ref, wkv_ref,
                q_ref, k_ref, v_ref, gb_ref, za_ref, zb_ref, xbc_ref, ac_ref,
                xrf_ref, xrb_ref, acxf_ref, acxb_ref, mq_ref, mk_ref, mv_ref):
    i = pl.program_id(1)
    tm = cur_ref.shape[1]
    body = slice(SUBLANE, SUBLANE + tm)
    first = i == 0
    last = i == pl.num_programs(1) - 1
    hx = jnp.concatenate([prev_ref[0], cur_ref[0], next_ref[0]], axis=0)
    u = _rms(hx, g_ref[...]).astype(BF16)

    def conv_cols(c0, w_ref, wc0):
        cols = _dot(u, wconv_ref[:, c0:c0 + CONV_GROUP_W])
        xe = jnp.concatenate([jnp.where(first, 0.0, cols[:SUBLANE]), cols[body],
                              jnp.where(last, 0.0, cols[SUBLANE + tm:])], axis=0)
        return _conv(xe, w_ref, wc0, CONV_GROUP_W)

    for c0, o_ref, scale in ((0, q_ref, GDN_DK ** -0.5), (GDN_QK_W, k_ref, 1.0)):
        a = _silu(conv_cols(c0, gconv_ref, c0))
        for h in range(GDN_HEADS):
            sl = slice(h * GDN_DK, (h + 1) * GDN_DK)
            seg = a[:, sl]
            o_ref[0, :, sl] = (seg * (lax.rsqrt(jnp.sum(seg * seg, axis=-1, keepdims=True) + EPS)
                                      * scale)).astype(BF16)
    v_ref[0] = _silu(conv_cols(2 * GDN_QK_W, gconv_ref, 2 * GDN_QK_W)).astype(BF16)

    small = _dot(u, wsmall_ref[...])[body]

    s = small[:, :LANE]
    lane = lax.broadcasted_iota(jnp.int32, s.shape, 1)
    is_gdn = lane < GDN_CHAINS
    is_ssd = (lane >= SSD_DT_LANE) & (lane < SSD_DT_LANE + 2 * SSD_HEADS)
    sp = _softplus(s + gbias_ref[...] + sbias_ref[...])
    log_decay = jnp.where(is_gdn | is_ssd, -jnp.exp(galog_ref[...] + salog_ref[...]) * sp, 0.0)
    fwd = (lane < GDN_HEADS) | ((lane >= SSD_DT_LANE) & (lane < SSD_DT_LANE + SSD_HEADS))
    csum = _chunk_cumsum(log_decay, fwd)
    gb_ref[0] = jnp.where(is_gdn, csum, jax.nn.sigmoid(s))
    ac_ref[0] = jnp.where(is_ssd, csum, 0.0)

    assert CONV_GROUP_W == SSD_INNER
    for c0 in range(0, SSD_XBC, CONV_GROUP_W):
        a = _silu(conv_cols(GDN_QKV_W + c0, sconv_ref, c0) + sconvb_ref[:, c0:c0 + CONV_GROUP_W])
        xbc_ref[0, :, c0:c0 + CONV_GROUP_W] = a.astype(BF16)
        if c0 == 0:
            for d, (xr_ref, acx_ref) in enumerate(((xrf_ref, acxf_ref), (xrb_ref, acxb_ref))):
                base = SSD_DT_LANE + d * SSD_HEADS
                xr_ref[0] = (a * _expand_heads(sp, base)).astype(BF16)
                acx_ref[0] = _expand_heads(csum, base)

    rest = _dot(u, wrest_ref[...])[body]
    za_ref[0] = rest[:, :GDN_V_W].astype(BF16)
    zb_ref[0] = rest[:, GDN_V_W:GDN_V_W + SSD_INNER].astype(BF16)

    c = rest[:, GDN_V_W + SSD_INNER:]
    cs = cs_ref[...]
    cq = _rms(c[:, :MLA_Q_LORA], qn_ref[...]).astype(BF16)
    ckv = _rms(c[:, MLA_Q_LORA:], kvn_ref[...]).astype(BF16)
    qa = _dot(cq, wq_ref[...]) * (MLA_SCALE * LOG2_E)
    kva = _dot(ckv, wkv_ref[...])
    k_rope = _rope_pair(small[:, LANE:], cs).astype(BF16)
    for h in range(MLA_HEADS):
        base = h * MLA_QK_PAD
        mq_ref[0, h, :, :MLA_NOPE] = qa[:, base:base + MLA_NOPE].astype(BF16)
        mq_ref[0, h, :, MLA_NOPE:] = _rope_pair(qa[:, base + MLA_NOPE:base + MLA_QK_PAD], cs).astype(BF16)
        mk_ref[0, h, :, :MLA_NOPE] = kva[:, base:base + MLA_NOPE].astype(BF16)
        mk_ref[0, h, :, MLA_NOPE:] = k_rope
        mv_ref[0, h, :, :MLA_V] = kva[:, base + MLA_NOPE:base + MLA_NOPE + MLA_V].astype(BF16)
        mv_ref[0, h, :, MLA_V:] = jnp.ones((tm, MLA_V_PAD - MLA_V), BF16)


def _mix(h3, cs, lp):
    B, L, _ = h3.shape
    tm = TM_MIX
    cur, prev, nxt = _seq_specs(tm, D_MODEL, L)
    out = lambda w: pl.BlockSpec((1, tm, w), lambda b, i: (b, i, 0))
    head_out = lambda w: pl.BlockSpec((1, MLA_HEADS, tm, w), lambda b, i: (b, 0, i, 0))
    seq_shape = lambda w, dt: jax.ShapeDtypeStruct((B, L, w), dt)
    head_shape = lambda w: jax.ShapeDtypeStruct((B, MLA_HEADS, L, w), BF16)
    consts = (lp["mix_norm"], lp["w_conv"], lp["w_rest"], lp["w_small"],
              lp["gdn_conv"], lp["gdn_alog"], lp["gdn_bias"],
              lp["ssd_conv"], lp["ssd_conv_b"], lp["ssd_alog"], lp["ssd_bias"],
              lp["mla_q_norm"], lp["mla_kv_norm"], lp["mla_wq"], lp["mla_wkv"])
    return pl.pallas_call(
        _mix_kernel,
        grid=(B, L // tm),
        in_specs=[cur, prev, nxt, pl.BlockSpec((tm, LANE), lambda b, i: (i, 0))]
        + [_const_spec(a.shape) for a in consts],
        out_specs=[out(GDN_QK_W), out(GDN_QK_W), out(GDN_V_W), out(LANE), out(GDN_V_W), out(SSD_INNER),
                   out(SSD_XBC), out(LANE),
                   out(SSD_INNER), out(SSD_INNER), out(SSD_INNER), out(SSD_INNER),
                   head_out(MLA_QK_PAD), head_out(MLA_QK_PAD), head_out(MLA_V_PAD)],
        out_shape=[seq_shape(GDN_QK_W, BF16), seq_shape(GDN_QK_W, BF16), seq_shape(GDN_V_W, BF16),
                   seq_shape(LANE, F32), seq_shape(GDN_V_W, BF16), seq_shape(SSD_INNER, BF16),
                   seq_shape(SSD_XBC, BF16), seq_shape(LANE, F32),
                   seq_shape(SSD_INNER, BF16), seq_shape(SSD_INNER, BF16),
                   seq_shape(SSD_INNER, F32), seq_shape(SSD_INNER, F32),
                   head_shape(MLA_QK_PAD), head_shape(MLA_QK_PAD), head_shape(MLA_V_PAD)],
        compiler_params=_cparams("parallel", "parallel"),
        name="mix",
    )(h3, h3, h3, cs, *consts)


def _seq_specs(tr, width, L):
    per = tr // SUBLANE
    last = L // SUBLANE - 1
    cur = pl.BlockSpec((1, tr, width), lambda b, i: (b, i, 0))
    prev = pl.BlockSpec((1, SUBLANE, width), lambda b, i: (b, jnp.maximum(i * per - 1, 0), 0))
    nxt = pl.BlockSpec((1, SUBLANE, width), lambda b, i: (b, jnp.minimum((i + 1) * per, last), 0))
    return cur, prev, nxt


def _conv(xe, w_ref, c0, width):
    n = xe.shape[0]
    groups = (n - 2 * SUBLANE) // SUBLANE
    x3 = xe.reshape(n // SUBLANE, SUBLANE, width)
    sub = lax.broadcasted_iota(jnp.int32, (groups, SUBLANE, width), 1)
    acc = None
    for k in range(CONV_W):
        s = CONV_HALF - k
        if s == 0:
            shifted = x3[1:1 + groups]
        else:
            r = pltpu.roll(x3, s % SUBLANE, 1)
            if s > 0:
                shifted = jnp.where(sub < s, r[0:groups], r[1:1 + groups])
            else:
                shifted = jnp.where(sub >= SUBLANE + s, r[2:2 + groups], r[1:1 + groups])
        term = shifted * w_ref[k:k + 1, c0:c0 + width]
        acc = term if acc is None else acc + term
    return acc.reshape(groups * SUBLANE, width)


def _chunk_cumsum(g, lane_is_fwd):
    tr = g.shape[0]
    pos = lax.broadcasted_iota(jnp.int32, g.shape, 0) & (CHUNK - 1)
    pre = g
    suf = g
    s = 1
    while s < CHUNK:
        pre = pre + jnp.where(pos >= s, pltpu.roll(pre, s, 0), 0.0)
        suf = suf + jnp.where(pos < CHUNK - s, pltpu.roll(suf, tr - s, 0), 0.0)
        s *= 2
    return jnp.where(lane_is_fwd, pre, suf)


NEUMANN_STEPS = 6
assert 2 ** NEUMANN_STEPS == CHUNK


def _gdn_scan_kernel(qf_ref, kf_ref, vf_ref, gbf_ref, grf_ref, qb_ref, kb_ref, vb_ref, gbb_ref, grb_ref,
                     of_ref, ob_ref, s_ref, *, nc):
    @pl.when(pl.program_id(1) == 0)
    def _():
        s_ref[...] = jnp.zeros_like(s_ref)

    dirs = ((qf_ref, kf_ref, vf_ref, gbf_ref, grf_ref, of_ref),
            (qb_ref, kb_ref, vb_ref, gbb_ref, grb_ref, ob_ref))
    ii = lax.broadcasted_iota(jnp.int32, (CHUNK, CHUNK), 0)
    jj = lax.broadcasted_iota(jnp.int32, (CHUNK, CHUNK), 1)
    eye = jnp.where(ii == jj, 1.0, 0.0)
    t_half = lax.broadcasted_iota(jnp.int32, (CHUNK, 2 * CHUNK), 1) >= CHUNK

    units = []
    for step_idx in range(nc):
        for d, (q_ref, k_ref, v_ref, gb_ref, gr_ref, _) in enumerate(dirs):
            cc = step_idx if d == 0 else nc - 1 - step_idx
            rows = slice(cc * CHUNK, (cc + 1) * CHUNK)
            gb = gb_ref[0, rows, :]
            grow_all = gr_ref[0, cc]
            for h in range(GDN_HEADS):
                j = d * GDN_HEADS + h
                sl = slice(h * GDN_DK, (h + 1) * GDN_DK)
                units.append(dict(d=d, j=j, rows=rows, sl=sl, q=q_ref[0, rows, sl], k=k_ref[0, rows, sl],
                                  v=v_ref[0, rows, sl], gcol=gb[:, j:j + 1], grow=grow_all[j:j + 1, :],
                                  bcol=gb[:, GDN_CHAINS + j:GDN_CHAINS + j + 1]))
    for t in units:
        r = _dot_nt(jnp.concatenate([t["k"], t["q"]], axis=0), t["k"])
        t["kk"] = r[:CHUNK]
        t["qk"] = r[CHUNK:]
    for t in units:
        gcol, bcol = t["gcol"], t["bcol"]
        incl = (ii >= jj) if t["d"] == 0 else (ii <= jj)
        dec = jnp.where(incl, jnp.exp(jnp.minimum(gcol - t["grow"], 0.0)), 0.0)
        nmat = -(bcol * t.pop("kk")) * jnp.where(ii == jj, 0.0, dec)
        t["nt"] = jnp.concatenate([nmat, eye], axis=1)
        t["attn"] = (t.pop("qk") * dec).astype(BF16)
        kf = t["k"].astype(F32)
        eg = jnp.exp(gcol)
        glast = gcol[CHUNK - 1:CHUNK] if t["d"] == 0 else gcol[0:1]
        t["rhs"] = (bcol * jnp.concatenate([t["v"].astype(F32), kf * eg], axis=1)).astype(BF16)
        t["qd"] = t["q"].astype(F32) * eg
        t["kd"] = (kf * jnp.exp(glast - gcol)).astype(BF16)
        t["eglast"] = jnp.exp(glast)
    for step in range(NEUMANN_STEPS):
        for t in units:
            ntb = t["nt"].astype(BF16)
            t["y"] = _dot(ntb[:, :CHUNK], ntb)
        for t in units:
            t["nt"] = t.pop("y") + jnp.where(t_half, t["nt"], 0.0)
    for t in units:
        x = _dot(t.pop("nt")[:, CHUNK:].astype(BF16), t.pop("rhs"))
        t["u"] = x[:, :GDN_DV]
        t["wq"] = jnp.concatenate([x[:, GDN_DV:], t.pop("qd")], axis=0).astype(BF16)

    state = [s_ref[j] for j in range(GDN_CHAINS)]
    for step_idx in range(nc):
        group = units[step_idx * GDN_CHAINS:(step_idx + 1) * GDN_CHAINS]
        for t in group:
            t["r"] = _dot(t["wq"], state[t["j"]].astype(BF16))
        for t in group:
            t["vnew"] = (t["u"] - t["r"][:CHUNK]).astype(BF16)
        for t in group:
            t["o"] = t["r"][CHUNK:] + _dot(t["attn"], t["vnew"])
            state[t["j"]] = state[t["j"]] * t["eglast"] + _dot_tn(t["kd"], t["vnew"])
        for t in group:
            dirs[t["d"]][5][0, t["rows"], t["sl"]] = t["o"]
    for j in range(GDN_CHAINS):
        s_ref[j] = state[j]


def _gdn_scan(q, k, v, gb, grow):
    B, L, _ = q.shape
    tr = TR_SEQ
    nc = tr // CHUNK
    nt = L // tr
    fwd = lambda w: pl.BlockSpec((1, tr, w), lambda b, i: (b, i, 0))
    bwd = lambda w: pl.BlockSpec((1, tr, w), lambda b, i: (b, nt - 1 - i, 0))
    grf = pl.BlockSpec((1, nc, GDN_CHAINS, CHUNK), lambda b, i: (b, i, 0, 0))
    grb = pl.BlockSpec((1, nc, GDN_CHAINS, CHUNK), lambda b, i: (b, nt - 1 - i, 0, 0))
    return pl.pallas_call(
        functools.partial(_gdn_scan_kernel, nc=nc),
        grid=(B, nt),
        in_specs=[fwd(GDN_QK_W), fwd(GDN_QK_W), fwd(GDN_V_W), fwd(LANE), grf,
                  bwd(GDN_QK_W), bwd(GDN_QK_W), bwd(GDN_V_W), bwd(LANE), grb],
        out_specs=[fwd(GDN_V_W), bwd(GDN_V_W)],
        out_shape=[jax.ShapeDtypeStruct((B, L, GDN_V_W), F32)] * 2,
        scratch_shapes=[pltpu.VMEM((GDN_CHAINS, GDN_DK, GDN_DV), F32)],
        compiler_params=_cparams("arbitrary", "arbitrary"),
        name="gdn_scan",
    )(q, k, v, gb, grow, q, k, v, gb, grow)


def _expand_heads(cols, base):
    rows = cols.shape[0]
    low_half = lax.broadcasted_iota(jnp.int32, (rows, LANE), 1) < SSD_HEAD_DIM
    tiles = []
    for h in range(0, SSD_HEADS, LANE // SSD_HEAD_DIM):
        tiles.append(jnp.where(low_half, cols[:, base + h:base + h + 1], cols[:, base + h + 1:base + h + 2]))
    return jnp.concatenate(tiles, axis=1)


def _ssd_scan_kernel(xf_ref, xrf_ref, acf_ref, arf_ref, xb_ref, xrb_ref, acb_ref, arb_ref,
                     yf_ref, yb_ref, h_ref, *, nc):
    @pl.when(pl.program_id(1) == 0)
    def _():
        h_ref[...] = jnp.zeros_like(h_ref)

    dirs = ((xf_ref, xrf_ref, acf_ref, arf_ref, yf_ref), (xb_ref, xrb_ref, acb_ref, arb_ref, yb_ref))
    ii = lax.broadcasted_iota(jnp.int32, (CHUNK, SSD_GROUP_W), 0)
    jj = lax.broadcasted_iota(jnp.int32, (CHUNK, SSD_GROUP_W), 1) & (CHUNK - 1)
    rb = lax.broadcasted_iota(jnp.int32, (SSD_GROUP_W, SSD_GROUP_W), 0) >> HEAD_SHIFT
    cb_ = lax.broadcasted_iota(jnp.int32, (SSD_GROUP_W, SSD_GROUP_W), 1) >> HEAD_SHIFT
    block_diag = rb == cb_

    units = []
    for step_idx in range(nc):
        for d, (x_ref, xr_ref, ac_ref, ar_ref, _) in enumerate(dirs):
            cc = step_idx if d == 0 else nc - 1 - step_idx
            rows = slice(cc * CHUNK, (cc + 1) * CHUNK)
            acx = ac_ref[0, rows, :]
            xr = xr_ref[0, rows, :].astype(F32)
            aclast = acx[CHUNK - 1:CHUNK] if d == 0 else acx[0:1]
            acrow = ar_ref[0, cc]
            for g in range(SSD_GROUPS):
                sl = slice(g * SSD_GROUP_W, (g + 1) * SSD_GROUP_W)
                b0 = SSD_INNER + g * SSD_STATE
                c0 = SSD_INNER + (SSD_GROUPS + g) * SSD_STATE
                units.append(dict(d=d, g=g, rows=rows, sl=sl, bg=x_ref[0, rows, b0:b0 + SSD_STATE],
                                  cg=x_ref[0, rows, c0:c0 + SSD_STATE], acx=acx[:, sl], acrow=acrow[:, sl],
                                  aclast=aclast[:, sl], xr=xr[:, sl]))
    for t in units:
        t["cbt"] = _dot_nt(t["cg"], jnp.concatenate([t["bg"]] * SSD_HEADS_PER_GROUP, axis=0))
    for t in units:
        incl = (ii >= jj) if t["d"] == 0 else (ii <= jj)
        seg = jnp.where(incl, jnp.exp(jnp.minimum(t["acx"] - t["acrow"], 0.0)), 0.0)
        m = (t.pop("cbt") * seg).astype(BF16)
        xr = t.pop("xr")
        xr_bd = jnp.where(block_diag, jnp.concatenate([xr] * SSD_HEADS_PER_GROUP, axis=0), 0.0).astype(BF16)
        t["ydiag"] = _dot(m, xr_bd)
        t["st"] = _dot_tn(t["bg"], (xr * jnp.exp(t["aclast"] - t["acx"])).astype(BF16))

    state = {(d, g): h_ref[d, g] for d in range(2) for g in range(SSD_GROUPS)}
    per_step = 2 * SSD_GROUPS
    for step_idx in range(nc):
        for t in units[step_idx * per_step:(step_idx + 1) * per_step]:
            key = (t["d"], t["g"])
            y = t["ydiag"] + _dot(t["cg"], state[key].astype(BF16)) * jnp.exp(t["acx"])
            dirs[t["d"]][4][0, t["rows"], t["sl"]] = y
            state[key] = state[key] * jnp.exp(t["aclast"]) + t["st"]
    for (d, g), hval in state.items():
        h_ref[d, g] = hval


def _ssd_scan(xbc, xr_f, xr_b, acx_f, acx_b, acrow_f, acrow_b):
    B, L, _ = xbc.shape
    tr = TR_SEQ
    nc = tr // CHUNK
    nt = L // tr
    fwd = lambda w: pl.BlockSpec((1, tr, w), lambda b, i: (b, i, 0))
    bwd = lambda w: pl.BlockSpec((1, tr, w), lambda b, i: (b, nt - 1 - i, 0))
    arf = pl.BlockSpec((1, nc, 1, SSD_INNER), lambda b, i: (b, i, 0, 0))
    arb = pl.BlockSpec((1, nc, 1, SSD_INNER), lambda b, i: (b, nt - 1 - i, 0, 0))
    return pl.pallas_call(
        functools.partial(_ssd_scan_kernel, nc=nc),
        grid=(B, nt),
        in_specs=[fwd(SSD_XBC), fwd(SSD_INNER), fwd(SSD_INNER), arf,
                  bwd(SSD_XBC), bwd(SSD_INNER), bwd(SSD_INNER), arb],
        out_specs=[fwd(SSD_INNER), bwd(SSD_INNER)],
        out_shape=[jax.ShapeDtypeStruct((B, L, SSD_INNER), F32)] * 2,
        scratch_shapes=[pltpu.VMEM((2, SSD_GROUPS, SSD_STATE, SSD_GROUP_W), F32)],
        compiler_params=_cparams("arbitrary", "arbitrary"),
        name="ssd_scan",
    )(xbc, xr_f, acx_f, acrow_f, xbc, xr_b, acx_b, acrow_b)


def _rope_pair(pair, cs):
    prod = pair * cs
    lane = lax.broadcasted_iota(jnp.int32, prod.shape, 1)
    return jnp.where(lane < MLA_ROPE, prod + pltpu.roll(prod, MLA_ROPE, 1), 0.0)


def _mla_attn_kernel(q_ref, k_ref, v_ref, o_ref):
    hs, tq = q_ref.shape[1], q_ref.shape[2]
    nk = k_ref.shape[2] // TK_ATTN
    steps = [(h, j) for h in range(hs) for j in range(nk)]

    def scores(h, j):
        return _dot_nt(q_ref[0, h], k_ref[0, h, j * TK_ATTN:(j + 1) * TK_ATTN, :])

    s_next = scores(*steps[0])
    for idx, (h, j) in enumerate(steps):
        s = s_next
        if idx + 1 < len(steps):
            s_next = scores(*steps[idx + 1])
        if j == 0:
            m = jnp.full((tq, 1), -jnp.inf, F32)
            acc = jnp.zeros((tq, MLA_V_PAD), F32)
        m_new = jnp.maximum(m, jnp.max(s, axis=-1, keepdims=True))
        p = jnp.exp2(s - m_new).astype(BF16)
        acc = jnp.exp2(m - m_new) * acc + _dot(p, v_ref[0, h, j * TK_ATTN:(j + 1) * TK_ATTN, :])
        m = m_new
        if j == nk - 1:
            o_ref[0, :, h * MLA_V:(h + 1) * MLA_V] = (acc[:, :MLA_V] / acc[:, MLA_V:]).astype(BF16)


def _mla_attn(q, k, v):
    B, H, L, _ = q.shape
    tq = TQ_ATTN
    kv_bytes_per_head = 2 * L * (MLA_QK_PAD + MLA_V_PAD) * 2
    hs = max(1, min(H, ATTN_KV_VMEM_BUDGET // kv_bytes_per_head))
    while H % hs:
        hs -= 1
    return pl.pallas_call(
        _mla_attn_kernel,
        grid=(B, H // hs, L // tq),
        in_specs=[pl.BlockSpec((1, hs, tq, MLA_QK_PAD), lambda b, h, i: (b, h, i, 0)),
                  pl.BlockSpec((1, hs, L, MLA_QK_PAD), lambda b, h, i: (b, h, 0, 0)),
                  pl.BlockSpec((1, hs, L, MLA_V_PAD), lambda b, h, i: (b, h, 0, 0))],
        out_specs=pl.BlockSpec((1, tq, hs * MLA_V), lambda b, h, i: (b, i, h)),
        out_shape=jax.ShapeDtypeStruct((B, L, H * MLA_V), BF16),
        compiler_params=_cparams("parallel", "parallel", "parallel"),
        name="mla_attn",
    )(q, k, v)


def _merge_kernel(h_ref, of_ref, ob_ref, za_ref, yf_ref, yb_ref, xbc_ref, zb_ref, oc_ref,
                  mixg_ref, gng_ref, dskip_ref, sng_ref, wa_ref, wb_ref, wc_ref, wg_ref, bg_ref, wo_ref,
                  out_ref):
    h = h_ref[...]
    u = _rms(h, mixg_ref[...]).astype(BF16)
    gates = jax.nn.sigmoid(_dot(u, wg_ref[...]) + bg_ref[...])

    o = of_ref[...] + ob_ref[...]
    za = za_ref[...].astype(F32)
    parts = []
    for hd in range(GDN_HEADS):
        sl = slice(hd * GDN_DV, (hd + 1) * GDN_DV)
        parts.append(_rms(o[:, sl], gng_ref[...]) * _silu(za[:, sl]))
    ya = _dot(jnp.concatenate(parts, axis=1).astype(BF16), wa_ref[...])

    y = yf_ref[...] + yb_ref[...] + xbc_ref[...].astype(F32) * dskip_ref[...]
    y = _rms(y * _silu(zb_ref[...].astype(F32)), sng_ref[...])
    yb = _dot(y.astype(BF16), wb_ref[...])

    yc = _dot(oc_ref[...], wc_ref[...])

    merged = (gates[:, :D_MODEL] * ya + gates[:, D_MODEL:2 * D_MODEL] * yb
              + gates[:, 2 * D_MODEL:] * yc)
    out_ref[...] = h + _dot(merged.astype(BF16), wo_ref[...])


def _merge(h2d, o_f, o_b, za, y_f, y_b, xbc_act, zb, oc, mix_gain, gdn_gain, dskip, ssd_gain,
           wa, wb, wc, wg, bg, wo):
    T = h2d.shape[0]
    tm = TM_MERGE
    row = lambda w: pl.BlockSpec((tm, w), lambda i: (i, 0))
    return pl.pallas_call(
        _merge_kernel,
        grid=(T // tm,),
        in_specs=[row(D_MODEL), row(GDN_V_W), row(GDN_V_W), row(GDN_V_W), row(SSD_INNER), row(SSD_INNER),
                  row(SSD_INNER), row(SSD_INNER), row(MLA_HEADS * MLA_V),
                  _const_spec((1, D_MODEL)), _const_spec((1, GDN_DV)), _const_spec((1, SSD_INNER)),
                  _const_spec((1, SSD_INNER)), _const_spec((GDN_V_W, D_MODEL)),
                  _const_spec((SSD_INNER, D_MODEL)), _const_spec((MLA_HEADS * MLA_V, D_MODEL)),
                  _const_spec((D_MODEL, N_BRANCH * D_MODEL)), _const_spec((1, N_BRANCH * D_MODEL)),
                  _const_spec((D_MODEL, D_MODEL))],
        out_specs=row(D_MODEL),
        out_shape=jax.ShapeDtypeStruct((T, D_MODEL), F32),
        compiler_params=_cparams("parallel"),
        name="merge",
    )(h2d, o_f, o_b, za, y_f, y_b, xbc_act, zb, oc, mix_gain, gdn_gain, dskip, ssd_gain,
      wa, wb, wc, wg, bg, wo)


def _rotate_half_cols(w):
    w1, w2 = jnp.split(w, 2, axis=-1)
    return jnp.concatenate([-w2, w1], axis=-1)


def _lane_row(values, base):
    row = jnp.zeros((1, LANE), F32)
    return row.at[0, base:base + values.shape[0]].set(values.astype(F32))


def _prep_layer(p, i):
    w_in = p["w_in"][i]
    c = 0
    seg = {}
    for name, width in (("qkv", GDN_QKV_W), ("za", GDN_V_W), ("ab", 4 * GDN_HEADS), ("zb", SSD_INNER),
                        ("xbc", SSD_XBC), ("dt", 2 * SSD_HEADS), ("cq", MLA_Q_LORA), ("ckv", MLA_KV_LORA),
                        ("kr", MLA_ROPE)):
        seg[name] = w_in[:, c:c + width]
        c += width
    w_conv = jnp.concatenate([seg["qkv"], seg["xbc"]], axis=1)
    w_rest = jnp.concatenate([seg["za"], seg["zb"], seg["cq"], seg["ckv"]], axis=1)
    pad = jnp.zeros((D_MODEL, LANE - 4 * GDN_HEADS - 2 * SSD_HEADS), F32)
    w_small = jnp.concatenate([seg["ab"], seg["dt"], pad, seg["kr"], _rotate_half_cols(seg["kr"])], axis=1)

    wq = p["mla_w_uq"][i].reshape(MLA_Q_LORA, MLA_HEADS, MLA_NOPE + MLA_ROPE)
    wq_rope = wq[..., MLA_NOPE:]
    wq = jnp.concatenate([wq, _rotate_half_cols(wq_rope)], axis=-1).reshape(MLA_Q_LORA, MLA_HEADS * MLA_QK_PAD)

    row = lambda v: v.reshape(1, -1).astype(F32)
    conv_pad = lambda w: jnp.concatenate([w, jnp.zeros((SUBLANE - CONV_W, w.shape[1]), F32)], axis=0)
    return dict(
        ffn1_norm=row(p["ffn1_norm"][i]), w_ffn1_in=p["w_ffn1_in"][i].astype(BF16),
        w_ffn1_out=p["w_ffn1_out"][i].astype(BF16),
        mix_norm=row(p["mix_norm"][i]), w_conv=w_conv.astype(BF16), w_rest=w_rest.astype(BF16),
        w_small=w_small.astype(BF16),
        gdn_conv=conv_pad(p["gdn_conv"][i]),
        gdn_alog=_lane_row(p["gdn_A_log"][i].reshape(-1), GDN_AB_LANE),
        gdn_bias=_lane_row(p["gdn_dt_bias"][i].reshape(-1), GDN_AB_LANE),
        gdn_norm=row(p["gdn_norm"][i]),
        ssd_conv=conv_pad(p["ssd_conv"][i]), ssd_conv_b=row(p["ssd_conv_b"][i]),
        ssd_alog=_lane_row(p["ssd_A_log"][i].reshape(-1), SSD_DT_LANE),
        ssd_bias=_lane_row(p["ssd_dt_bias"][i].reshape(-1), SSD_DT_LANE),
        ssd_dskip=row(jnp.repeat(p["ssd_D"][i], SSD_HEAD_DIM)), ssd_norm=row(p["ssd_norm"][i]),
        mla_q_norm=row(p["mla_q_norm"][i]), mla_kv_norm=row(p["mla_kv_norm"][i]),
        mla_wq=wq.astype(BF16), mla_wkv=p["mla_w_ukv"][i].astype(BF16),
        w_branch_a=p["w_branch_a"][i].astype(BF16), w_branch_b=p["w_branch_b"][i].astype(BF16),
        w_branch_c=p["w_branch_c"][i].astype(BF16), w_gate=p["w_gate"][i].astype(BF16),
        b_gate=row(p["b_gate"][i]), w_out=p["w_out"][i].astype(BF16),
        ffn2_norm=row(p["ffn2_norm"][i]), w_ffn2_in=p["w_ffn2_in"][i].astype(BF16),
        w_ffn2_out=p["w_ffn2_out"][i].astype(BF16),
    )


def _rope_table(L):
    inv_freq = jnp.power(ROPE_BASE, -jnp.arange(0, MLA_ROPE, 2, dtype=F32) / MLA_ROPE)
    ang = jnp.arange(L, dtype=F32)[:, None] * inv_freq[None, :]
    ang = jnp.concatenate([ang, ang], axis=-1)
    return jnp.concatenate([jnp.cos(ang), jnp.sin(ang)], axis=-1)


def _rows_to_chunk_rows(cols, B, L):
    n = cols.shape[-1]
    return cols.reshape(B, L // CHUNK, CHUNK, n).transpose(0, 1, 3, 2)


def _layer(x2d, B, L, cs, lp, final_gain, final):
    T = B * L
    h = _ffn(x2d, lp["ffn1_norm"], lp["w_ffn1_in"], lp["w_ffn1_out"], final_gain, False)
    (q, k, v, gb, za, zb, xbc_act, ac, xr_f, xr_b, acx_f, acx_b,
     qc, kc, vc) = _mix(h.reshape(B, L, D_MODEL), cs, lp)

    grow = _rows_to_chunk_rows(gb[..., :GDN_CHAINS], B, L)
    o_f, o_b = _gdn_scan(q, k, v, gb, grow)

    acrow = _rows_to_chunk_rows(ac[..., SSD_DT_LANE:SSD_DT_LANE + 2 * SSD_HEADS], B, L)
    acrow = acrow.reshape(B, L // CHUNK, 2, 1, SSD_HEADS, 1, CHUNK)
    acrow = jnp.broadcast_to(acrow, (B, L // CHUNK, 2, 1, SSD_HEADS, SSD_HEAD_DIM // CHUNK, CHUNK))
    acrow = acrow.reshape(B, L // CHUNK, 2, 1, SSD_INNER)
    y_f, y_b = _ssd_scan(xbc_act, xr_f, xr_b, acx_f, acx_b, acrow[:, :, 0], acrow[:, :, 1])

    oc = _mla_attn(qc, kc, vc)

    flat = lambda a: a.reshape(T, a.shape[-1])
    h = _merge(h, flat(o_f), flat(o_b), flat(za), flat(y_f), flat(y_b), flat(xbc_act), flat(zb), flat(oc),
               lp["mix_norm"], lp["gdn_norm"], lp["ssd_dskip"], lp["ssd_norm"],
               lp["w_branch_a"], lp["w_branch_b"], lp["w_branch_c"], lp["w_gate"], lp["b_gate"], lp["w_out"])
    return _ffn(h, lp["ffn2_norm"], lp["w_ffn2_in"], lp["w_ffn2_out"], final_gain, final)


def _trunk(x, layers, final_gain):
    B, L, _ = x.shape
    cs = _rope_table(L)
    h = x.reshape(B * L, D_MODEL)
    for i, lp in enumerate(layers):
        h = _layer(h, B, L, cs, lp, final_gain, i == len(layers) - 1)
    return h.reshape(B, L, D_MODEL)


def kernel(x_prompt, x_sample, ffn1_norm, w_ffn1_in, w_ffn1_out, mix_norm, w_in, gdn_conv, gdn_A_log, gdn_dt_bias, gdn_norm, ssd_conv, ssd_conv_b, ssd_A_log, ssd_dt_bias, ssd_D, ssd_norm, mla_q_norm, mla_w_uq, mla_kv_norm, mla_w_ukv, w_branch_a, w_branch_b, w_branch_c, w_gate, b_gate, w_out, ffn2_norm, w_ffn2_in, w_ffn2_out, final_norm):
    p = dict(ffn1_norm=ffn1_norm, w_ffn1_in=w_ffn1_in, w_ffn1_out=w_ffn1_out, mix_norm=mix_norm, w_in=w_in,
             gdn_conv=gdn_conv, gdn_A_log=gdn_A_log, gdn_dt_bias=gdn_dt_bias, gdn_norm=gdn_norm,
             ssd_conv=ssd_conv, ssd_conv_b=ssd_conv_b, ssd_A_log=ssd_A_log, ssd_dt_bias=ssd_dt_bias,
             ssd_D=ssd_D, ssd_norm=ssd_norm, mla_q_norm=mla_q_norm, mla_w_uq=mla_w_uq,
             mla_kv_norm=mla_kv_norm, mla_w_ukv=mla_w_ukv, w_branch_a=w_branch_a, w_branch_b=w_branch_b,
             w_branch_c=w_branch_c, w_gate=w_gate, b_gate=b_gate, w_out=w_out, ffn2_norm=ffn2_norm,
             w_ffn2_in=w_ffn2_in, w_ffn2_out=w_ffn2_out)
    layers = [_prep_layer(p, i) for i in range(DEPTH)]
    final_gain = final_norm.reshape(1, D_MODEL).astype(F32)
    return (_trunk(x_prompt, layers, final_gain), _trunk(x_sample, layers, final_gain))
```
